```python
import math
import jax, jax.numpy as jnp
from jax import lax
import numpy as np

D_MODEL = 1024
BATCH = 32
SEQ = 2048
DEPTH = 2

N_MIXERS = 2
N_A_LAYERS = (DEPTH + 1) // 2
N_B_LAYERS = DEPTH // 2
N_META = 16
D_RNN = D_MODEL
LRU_BLOCKS = 8
LRU_BLOCK_W = D_RNN // LRU_BLOCKS
LRU_CONV_W = 4
LRU_C = 8.0
CONF_KERNEL = 31
PEER_HEADS = 8
PEER_N_KEYS = 128
PEER_N_EXPERTS = PEER_N_KEYS * PEER_N_KEYS
PEER_QUERY_DIM = 256
PEER_HALF = PEER_QUERY_DIM // 2
PEER_TOPK = 16
PEER_CHUNK = 256
DEEPNORM_ALPHA = (2.0 * DEPTH) ** 0.25
DEEPNORM_BETA = (8.0 * DEPTH) ** -0.25
LN_EPS = 1e-5

kernel_name = "hybrid_rglru_conformer_peer_encoder"


def layer_norm(x, g, b):
    xf = x.astype(jnp.float32)
    mu = jnp.mean(xf, axis=-1, keepdims=True)
    var = jnp.mean(jnp.square(xf - mu), axis=-1, keepdims=True)
    y = (xf - mu) * lax.rsqrt(var + LN_EPS)
    return (y * g.astype(jnp.float32) + b.astype(jnp.float32)).astype(x.dtype)


def depthwise_conv(x, w, b, pad_left, pad_right):
    k, c = w.shape
    y = lax.conv_general_dilated(
        x, w.reshape(k, 1, c).astype(x.dtype), window_strides=(1,),
        padding=[(pad_left, pad_right)], dimension_numbers=('NWC', 'WIO', 'NWC'),
        feature_group_count=c)
    return y + b.astype(x.dtype)


def rglru_coeffs(u, w_a, b_a, w_x, b_x, lam):
    bsz, s, _ = u.shape
    ub = u.reshape(bsz, s, LRU_BLOCKS, LRU_BLOCK_W)
    r = jax.nn.sigmoid(jnp.einsum('bsnc,ncd->bsnd', ub, w_a).reshape(bsz, s, D_RNN) + b_a)
    i = jax.nn.sigmoid(jnp.einsum('bsnc,ncd->bsnd', ub, w_x).reshape(bsz, s, D_RNN) + b_x)
    log_a = (-LRU_C * r.astype(jnp.float32)) * jax.nn.softplus(-lam.astype(jnp.float32))
    a = jnp.exp(log_a)
    mult = jnp.sqrt(-jnp.expm1(2.0 * log_a))
    return a, mult * (i * u).astype(jnp.float32)


def linear_scan(a, b):
    def combine(c1, c2):
        a1, b1 = c1
        a2, b2 = c2
        return a1 * a2, a2 * b1 + b2
    _, h = lax.associative_scan(combine, (a, b), axis=1)
    return h


def rglru_mixer(x, w_in, conv_w, conv_b, w_a, b_a, w_x, b_x, lam, w_out):
    gate, u = jnp.split(x @ w_in, 2, axis=-1)
    pl = LRU_CONV_W // 2
    u = depthwise_conv(u, conv_w, conv_b, pl, LRU_CONV_W - 1 - pl)
    a_f, b_f = rglru_coeffs(u, w_a[0], b_a[0], w_x[0], b_x[0], lam[0])
    a_r, b_r = rglru_coeffs(u, w_a[1], b_a[1], w_x[1], b_x[1], lam[1])
    h_f = linear_scan(a_f, b_f)
    h_r = jnp.flip(linear_scan(jnp.flip(a_r, axis=1), jnp.flip(b_r, axis=1)), axis=1)
    h = (h_f + h_r).astype(x.dtype)
    return (h * jax.nn.gelu(gate)) @ w_out


def conformer_conv_mixer(x, w_pw1, b_pw1, dw_w, dw_b, ln_g, ln_b, w_pw2, b_pw2):
    h = jax.nn.glu(x @ w_pw1 + b_pw1, axis=-1)
    h = depthwise_conv(h, dw_w, dw_b, CONF_KERNEL // 2, CONF_KERNEL // 2)
    h = jax.nn.silu(layer_norm(h, ln_g, ln_b))
    return h @ w_pw2 + b_pw2


def peer_block(xc, w_query, sub_keys, expert_u, expert_v):
    c = xc.shape[0]
    q = (xc @ w_query).reshape(c, PEER_HEADS, 2, PEER_HALF)
    scores = jnp.einsum('chpk,pnk->chpn', q, sub_keys.astype(q.dtype))
    s1, i1 = lax.top_k(scores[:, :, 0], PEER_TOPK)
    s2, i2 = lax.top_k(scores[:, :, 1], PEER_TOPK)
    cand = (s1[..., :, None] + s2[..., None, :]).reshape(c, PEER_HEADS, PEER_TOPK * PEER_TOPK)
    s, flat = lax.top_k(cand, PEER_TOPK)
    idx1 = jnp.take_along_axis(i1, flat // PEER_TOPK, axis=-1)
    idx2 = jnp.take_along_axis(i2, flat % PEER_TOPK, axis=-1)
    expert = idx1 * PEER_N_KEYS + idx2
    g = jax.nn.softmax(s.astype(jnp.float32), axis=-1).astype(xc.dtype)
    u = expert_u[expert]
    act = jax.nn.gelu(jnp.einsum('chkd,cd->chk', u, xc))
    v = expert_v[expert]
    return jnp.einsum('chk,chkd->cd', g * act, v)


def peer_ffn(x, w_query, sub_keys, expert_u, expert_v):
    bsz, s, d = x.shape
    t = bsz * s
    n_blocks = -(-t // PEER_CHUNK)
    xt = jnp.pad(x.reshape(t, d), ((0, n_blocks * PEER_CHUNK - t), (0, 0)))
    xt = xt.reshape(n_blocks, PEER_CHUNK, d)
    y = lax.map(lambda xc: peer_block(xc, w_query, sub_keys, expert_u, expert_v), xt)
    return y.reshape(n_blocks * PEER_CHUNK, d)[:t].reshape(bsz, s, d)


def setup_inputs(seed: int = 0) -> dict:
    key = jax.random.key(seed)
    ks = jax.random.split(key, 32)
    f32 = jnp.float32
    nrm = lambda k, shape, scale: jax.random.normal(k, shape, f32) * scale
    a_init = jax.random.uniform(ks[8], (N_A_LAYERS, 2, D_RNN), f32, 0.9, 0.999)
    base = a_init ** (1.0 / LRU_C)
    lam = jnp.log(base) - jnp.log1p(-base)
    return {
        "x": nrm(ks[0], (BATCH, SEQ, D_MODEL), 1.0),
        "meta_tokens": nrm(ks[1], (N_META, D_MODEL), 1.0),
        "lru_w_in": nrm(ks[2], (N_A_LAYERS, D_MODEL, 2 * D_RNN), D_MODEL ** -0.5),
        "lru_conv_w": nrm(ks[3], (N_A_LAYERS, LRU_CONV_W, D_RNN), LRU_CONV_W ** -0.5),
        "lru_conv_b": nrm(ks[4], (N_A_LAYERS, D_RNN), 0.01),
        "lru_w_a": nrm(ks[5], (N_A_LAYERS, 2, LRU_BLOCKS, LRU_BLOCK_W, LRU_BLOCK_W), LRU_BLOCK_W ** -0.5),
        "lru_b_a": nrm(ks[6], (N_A_LAYERS, 2, D_RNN), 0.01),
        "lru_w_x": nrm(ks[7], (N_A_LAYERS, 2, LRU_BLOCKS, LRU_BLOCK_W, LRU_BLOCK_W), LRU_BLOCK_W ** -0.5),
        "lru_b_x": nrm(ks[9], (N_A_LAYERS, 2, D_RNN), 0.01),
        "lru_lambda": lam,
        "lru_w_out": nrm(ks[10], (N_A_LAYERS, D_RNN, D_MODEL), DEEPNORM_BETA * D_RNN ** -0.5),
        "conf_w_pw1": nrm(ks[11], (N_B_LAYERS, D_MODEL, 2 * D_MODEL), D_MODEL ** -0.5),
        "conf_b_pw1": nrm(ks[12], (N_B_LAYERS, 2 * D_MODEL), 0.01),
        "conf_dw_w": nrm(ks[13], (N_B_LAYERS, CONF_KERNEL, D_MODEL), CONF_KERNEL ** -0.5),
        "conf_dw_b": nrm(ks[14], (N_B_LAYERS, D_MODEL), 0.01),
        "conf_ln_g": 1.0 + nrm(ks[15], (N_B_LAYERS, D_MODEL), 0.01),
        "conf_ln_b": nrm(ks[16], (N_B_LAYERS, D_MODEL), 0.01),
        "conf_w_pw2": nrm(ks[17], (N_B_LAYERS, D_MODEL, D_MODEL), DEEPNORM_BETA * D_MODEL ** -0.5),
        "conf_b_pw2": nrm(ks[18], (N_B_LAYERS, D_MODEL), 0.01),
        "peer_w_query": nrm(ks[19], (DEPTH, D_MODEL, PEER_HEADS * PEER_QUERY_DIM), D_MODEL ** -0.5),
        "peer_sub_keys": nrm(ks[20], (DEPTH, 2, PEER_N_KEYS, PEER_HALF), PEER_HALF ** -0.5),
        "peer_u": nrm(ks[21], (DEPTH, PEER_N_EXPERTS, D_MODEL), D_MODEL ** -0.5),
        "peer_v": nrm(ks[22], (DEPTH, PEER_N_EXPERTS, D_MODEL), DEEPNORM_BETA * PEER_HEADS ** -0.5),
        "ln_mix_g": 1.0 + nrm(ks[23], (DEPTH, D_MODEL), 0.01),
        "ln_mix_b": nrm(ks[24], (DEPTH, D_MODEL), 0.01),
        "ln_ffn_g": 1.0 + nrm(ks[25], (DEPTH, D_MODEL), 0.01),
        "ln_ffn_b": nrm(ks[26], (DEPTH, D_MODEL), 0.01),
    }


def reference(x, meta_tokens, lru_w_in, lru_conv_w, lru_conv_b, lru_w_a, lru_b_a, lru_w_x,
              lru_b_x, lru_lambda, lru_w_out, conf_w_pw1, conf_b_pw1, conf_dw_w, conf_dw_b,
              conf_ln_g, conf_ln_b, conf_w_pw2, conf_b_pw2, peer_w_query, peer_sub_keys,
              peer_u, peer_v, ln_mix_g, ln_mix_b, ln_ffn_g, ln_ffn_b):
    bsz = x.shape[0]
    meta = jnp.broadcast_to(meta_tokens.astype(x.dtype)[None], (bsz, N_META, D_MODEL))
    h = jnp.concatenate([meta, x], axis=1)
    for i in range(DEPTH):
        j = i // N_MIXERS
        if i % N_MIXERS == 0:
            m = rglru_mixer(h, lru_w_in[j], lru_conv_w[j], lru_conv_b[j], lru_w_a[j], lru_b_a[j],
                            lru_w_x[j], lru_b_x[j], lru_lambda[j], lru_w_out[j])
        else:
            m = conformer_conv_mixer(h, conf_w_pw1[j], conf_b_pw1[j], conf_dw_w[j], conf_dw_b[j],
                                     conf_ln_g[j], conf_ln_b[j], conf_w_pw2[j], conf_b_pw2[j])
        h = layer_norm(DEEPNORM_ALPHA * h + m, ln_mix_g[i], ln_mix_b[i])
        f = peer_ffn(h, peer_w_query[i], peer_sub_keys[i], peer_u[i], peer_v[i])
        h = layer_norm(DEEPNORM_ALPHA * h + f, ln_ffn_g[i], ln_ffn_b[i])
    return h[:, N_META:]
```

```python
import functools

import jax
import jax.numpy as jnp
import numpy as np
from jax import lax
from jax.experimental import pallas as pl
from jax.experimental.pallas import tpu as pltpu

F32 = jnp.float32
BF16 = jnp.bfloat16
HIGHEST = lax.Precision.HIGHEST

N_META_TOKENS = 16
LRU_C = 8.0
LN_EPS = 1e-5
DEPTH = 2
DEEPNORM_ALPHA = (2.0 * DEPTH) ** 0.25

SUBLANES = 8
LANES = 128
VMEM_LIMIT_BYTES = 56 * 1024 * 1024

TOKEN_TILE = 512
SCAN_STEPS = 48
SCAN_CH = 256
PEER_ROUTE_TILE = 256
PEER_TOKEN_TILE = 64
PEER_TOPK = 16


def _params(semantics):
    return pltpu.CompilerParams(dimension_semantics=semantics, vmem_limit_bytes=VMEM_LIMIT_BYTES)


def _mxu(a, b):
    return jnp.dot(a.astype(BF16), b.astype(BF16), preferred_element_type=F32)


def _layer_norm(y, g, b):
    mu = jnp.mean(y, axis=-1, keepdims=True)
    d = y - mu
    var = jnp.mean(d * d, axis=-1, keepdims=True)
    return d * lax.rsqrt(var + LN_EPS) * g + b


def _expm1(x):
    series = x * (1.0 + x * (1 / 2 + x * (1 / 6 + x * (1 / 24 + x * (1 / 120 + x * (1 / 720))))))
    return jnp.where(jnp.abs(x) < 0.1, series, jnp.exp(x) - 1.0)


def _row_spec(tm, n, col=0):
    return pl.BlockSpec((tm, n), lambda i: (i, col))


def _full_spec(shape):
    nd = len(shape)
    return pl.BlockSpec(shape, lambda i: (0,) * nd)


def _mm_kernel(x_ref, w_ref, o_ref):
    o_ref[...] = _mxu(x_ref[...], w_ref[...])


def _matmul(x, w):
    t, k = x.shape
    n = w.shape[1]
    return pl.pallas_call(
        _mm_kernel,
        grid=(t // TOKEN_TILE,),
        in_specs=[_row_spec(TOKEN_TILE, k), _full_spec((k, n))],
        out_specs=_row_spec(TOKEN_TILE, n),
        out_shape=jax.ShapeDtypeStruct((t, n), F32),
        compiler_params=_params(("parallel",)),
        name="matmul",
    )(x, w)


def _mm_glu_kernel(x_ref, w_ref, b_ref, o_ref):
    z = _mxu(x_ref[...], w_ref[...]) + b_ref[...]
    d = o_ref.shape[1]
    o_ref[...] = z[:, :d] * jax.nn.sigmoid(z[:, d:])


def _matmul_glu(x, w, b):
    t, k = x.shape
    n = w.shape[1]
    return pl.pallas_call(
        _mm_glu_kernel,
        grid=(t // TOKEN_TILE,),
        in_specs=[_row_spec(TOKEN_TILE, k), _full_spec((k, n)), _full_spec((1, n))],
        out_specs=_row_spec(TOKEN_TILE, n // 2),
        out_shape=jax.ShapeDtypeStruct((t, n // 2), F32),
        compiler_params=_params(("parallel",)),
        name="matmul_glu",
    )(x, w, b)


def _mm_res_ln_kernel(a_ref, w_ref, b_ref, res_ref, g_ref, beta_ref, o_ref):
    y = DEEPNORM_ALPHA * res_ref[...] + _mxu(a_ref[...], w_ref[...]) + b_ref[...]
    o_ref[...] = _layer_norm(y, g_ref[...], beta_ref[...])


def _matmul_res_ln(a, w, b, res, g, beta):
    t, k = a.shape
    n = w.shape[1]
    return pl.pallas_call(
        _mm_res_ln_kernel,
        grid=(t // TOKEN_TILE,),
        in_specs=[_row_spec(TOKEN_TILE, k), _full_spec((k, n)), _full_spec((1, n)),
                  _row_spec(TOKEN_TILE, n), _full_spec((1, n)), _full_spec((1, n))],
        out_specs=_row_spec(TOKEN_TILE, n),
        out_shape=jax.ShapeDtypeStruct((t, n), F32),
        compiler_params=_params(("parallel",)),
        name="matmul_res_ln",
    )(a, w, b, res, g, beta)


def _lru_coef_kernel(u_ref, prev_ref, next_ref, cw_ref, cb_ref, wa_ref, ba_ref, wx_ref, bx_ref,
                     lam_ref, af_ref, bf_ref, ar_ref, br_ref, *, batch):
    i = pl.program_id(0)
    last = pl.num_programs(0) - 1
    tm = u_ref.shape[0]
    prev = jnp.where(i > 0, prev_ref[...], 0.0)
    nxt = jnp.where(i < last, next_ref[...], 0.0)
    ext = jnp.concatenate([prev, u_ref[...], nxt], axis=0)
    uc = cb_ref[...] + sum(cw_ref[k:k + 1, :] * ext[k * batch:k * batch + tm, :] for k in range(4))
    n_blocks = wa_ref.shape[1]
    bw = wa_ref.shape[2]
    outs = ((af_ref, bf_ref), (ar_ref, br_ref))
    for n in range(n_blocks):
        sl = slice(n * bw, (n + 1) * bw)
        ub = uc[:, sl]
        ub16 = ub.astype(BF16)
        for d in range(2):
            r = jax.nn.sigmoid(jnp.dot(ub16, wa_ref[d, n], preferred_element_type=F32) + ba_ref[d:d + 1, sl])
            g = jax.nn.sigmoid(jnp.dot(ub16, wx_ref[d, n], preferred_element_type=F32) + bx_ref[d:d + 1, sl])
            log_a = (-LRU_C * r) * jax.nn.softplus(-lam_ref[d:d + 1, sl])
            a_ref, b_ref = outs[d]
            a_ref[:, sl] = jnp.exp(log_a)
            b_ref[:, sl] = jnp.sqrt(-_expm1(2.0 * log_a)) * (g * ub)


def _lru_coefs(gu, conv_w, conv_b, w_a, b_a, w_x, b_x, lam, batch):
    t = gu.shape[0]
    d = conv_w.shape[1]
    tm = TOKEN_TILE
    n_prev = tm // (2 * batch)
    n_next = tm // batch
    last_next = t // batch - 1
    out = jax.ShapeDtypeStruct((t, d), F32)
    return pl.pallas_call(
        functools.partial(_lru_coef_kernel, batch=batch),
        grid=(t // tm,),
        in_specs=[
            pl.BlockSpec((tm, d), lambda i: (i, 1)),
            pl.BlockSpec((2 * batch, d), lambda i: (jnp.maximum(i * n_prev - 1, 0), 1)),
            pl.BlockSpec((batch, d), lambda i: (jnp.minimum((i + 1) * n_next, last_next), 1)),
            _full_spec(conv_w.shape), _full_spec((1, d)),
            _full_spec(w_a.shape), _full_spec(b_a.shape),
            _full_spec(w_x.shape), _full_spec(b_x.shape), _full_spec(lam.shape),
        ],
        out_specs=[_row_spec(tm, d)] * 4,
        out_shape=[out] * 4,
        compiler_params=_params(("parallel",)),
        name="lru_coefs",
    )(gu, gu, gu, conv_w, conv_b, w_a, b_a, w_x, b_x, lam)


def _lru_scan_kernel(af_ref, bf_ref, ar_ref, br_ref, hf_ref, hr_ref, cf_ref, cr_ref, *, batch):
    j = pl.program_id(1)

    @pl.when(j == 0)
    def _():
        cf_ref[...] = jnp.zeros_like(cf_ref)
        cr_ref[...] = jnp.zeros_like(cr_ref)

    steps = af_ref.shape[0] // batch

    def body(s, carry):
        hf, hr = carry
        rf = pl.multiple_of(s * batch, batch)
        hf = af_ref[pl.ds(rf, batch), :] * hf + bf_ref[pl.ds(rf, batch), :]
        hf_ref[pl.ds(rf, batch), :] = hf
        rr = pl.multiple_of((steps - 1 - s) * batch, batch)
        hr = ar_ref[pl.ds(rr, batch), :] * hr + br_ref[pl.ds(rr, batch), :]
        hr_ref[pl.ds(rr, batch), :] = hr
        return hf, hr

    hf, hr = lax.fori_loop(0, steps, body, (cf_ref[...], cr_ref[...]), unroll=4)
    cf_ref[...] = hf
    cr_ref[...] = hr


def _lru_scan(a_f, b_f, a_r, b_r, batch):
    t, d = a_f.shape
    rows = SCAN_STEPS * batch
    n_chunks = t // rows
    fwd = pl.BlockSpec((rows, SCAN_CH), lambda c, j: (j, c))
    rev = pl.BlockSpec((rows, SCAN_CH), lambda c, j: (n_chunks - 1 - j, c))
    out = jax.ShapeDtypeStruct((t, d), F32)
    return pl.pallas_call(
        functools.partial(_lru_scan_kernel, batch=batch),
        grid=(d // SCAN_CH, n_chunks),
        in_specs=[fwd, fwd, rev, rev],
        out_specs=[fwd, rev],
        out_shape=[out, out],
        scratch_shapes=[pltpu.VMEM((batch, SCAN_CH), F32), pltpu.VMEM((batch, SCAN_CH), F32)],
        compiler_params=_params(("parallel", "arbitrary")),
        name="lru_scan",
    )(a_f, b_f, a_r, b_r)


def _lru_out_kernel(hf_ref, hr_ref, gate_ref, w_ref, res_ref, g_ref, beta_ref, o_ref):
    a = (hf_ref[...] + hr_ref[...]) * jax.nn.gelu(gate_ref[...])
    y = DEEPNORM_ALPHA * res_ref[...] + _mxu(a, w_ref[...])
    o_ref[...] = _layer_norm(y, g_ref[...], beta_ref[...])


def _lru_out(h_f, h_r, gu, w_out, res, g, beta):
    t, d = h_f.shape
    n = w_out.shape[1]
    tm = TOKEN_TILE
    return pl.pallas_call(
        _lru_out_kernel,
        grid=(t // tm,),
        in_specs=[_row_spec(tm, d), _row_spec(tm, d), _row_spec(tm, d, col=0), _full_spec((d, n)),
                  _row_spec(tm, n), _full_spec((1, n)), _full_spec((1, n))],
        out_specs=_row_spec(tm, n),
        out_shape=jax.ShapeDtypeStruct((t, n), F32),
        compiler_params=_params(("parallel",)),
        name="lru_out",
    )(h_f, h_r, gu, w_out, res, g, beta)


def _dwconv_kernel(cur_ref, prev_ref, next_ref, w_ref, b_ref, g_ref, beta_ref, o_ref, ext_ref, *, batch):
    i = pl.program_id(0)
    last = pl.num_programs(0) - 1
    tm = cur_ref.shape[0]
    taps = w_ref.shape[0]
    halo = (taps // 2) * batch
    ext_ref[0:halo, :] = jnp.where(i > 0, prev_ref[tm - halo:tm, :], 0.0)
    ext_ref[halo:halo + tm, :] = cur_ref[...]
    ext_ref[halo + tm:halo + tm + halo, :] = jnp.where(i < last, next_ref[0:halo, :], 0.0)

    def body(c, _):
        r0 = pl.multiple_of(c * batch, batch)
        acc = jnp.zeros((batch, cur_ref.shape[1]), F32) + b_ref[...]
        for k in range(taps):
            acc = acc + w_ref[k:k + 1, :] * ext_ref[pl.ds(r0 + k * batch, batch), :]
        y = _layer_norm(acc, g_ref[...], beta_ref[...])
        o_ref[pl.ds(r0, batch), :] = y * jax.nn.sigmoid(y)
        return 0

    lax.fori_loop(0, tm // batch, body, 0)


def _dwconv_ln_silu(x, w, b, g, beta, batch):
    t, d = x.shape
    tm = TOKEN_TILE
    n_tiles = t // tm
    halo = (w.shape[0] // 2) * batch
    assert halo <= tm
    return pl.pallas_call(
        functools.partial(_dwconv_kernel, batch=batch),
        grid=(n_tiles,),
        in_specs=[
            _row_spec(tm, d),
            pl.BlockSpec((tm, d), lambda i: (jnp.maximum(i - 1, 0), 0)),
            pl.BlockSpec((tm, d), lambda i: (jnp.minimum(i + 1, n_tiles - 1), 0)),
            _full_spec(w.shape), _full_spec((1, d)), _full_spec((1, d)), _full_spec((1, d)),
        ],
        out_specs=_row_spec(tm, d),
        out_shape=jax.ShapeDtypeStruct((t, d), F32),
        scratch_shapes=[pltpu.VMEM((tm + 2 * halo, d), F32)],
        compiler_params=_params(("parallel",)),
        name="dwconv_ln_silu",
    )(x, x, x, w, b, g, beta)


def _top16(s, lane, width):
    vals = jnp.zeros(s.shape[:1] + (LANES,), F32)
    idxs = jnp.zeros(s.shape[:1] + (LANES,), jnp.int32)
    lane_out = lane[:, :LANES]
    for r in range(PEER_TOPK):
        m = jnp.max(s, axis=-1, keepdims=True)
        am = jnp.min(jnp.where(s == m, lane, width), axis=-1, keepdims=True)
        s = jnp.where(lane == am, -jnp.inf, s)
        vals = jnp.where(lane_out == r, m, vals)
        idxs = jnp.where(lane_out == r, am, idxs)
    return vals, idxs


def _route_kernel(q_ref, keys_ref, e1_ref, e2_ref, blk_ref, row_ref, hi_ref, g_ref,
                  s_acc, e_acc, m_acc, *, n_keys, half_experts_log2):
    h = pl.program_id(1)
    tb = q_ref.shape[0]
    lane = lax.broadcasted_iota(jnp.int32, (tb, LANES), 1)
    lane2 = lax.broadcasted_iota(jnp.int32, (tb, 2 * LANES), 1)

    @pl.when(h == 0)
    def _():
        s_acc[...] = jnp.zeros_like(s_acc)
        e_acc[...] = jnp.zeros_like(e_acc)
        m_acc[...] = jnp.zeros_like(m_acc)

    half = keys_ref.shape[1]
    q = q_ref[...].astype(BF16)
    v1, i1 = _top16(jnp.dot(q[:, :half], keys_ref[0], preferred_element_type=F32), lane, LANES)
    v2, i2 = _top16(jnp.dot(q[:, half:], keys_ref[1], preferred_element_type=F32), lane, LANES)
    cand = (jnp.dot(v1, e1_ref[...], precision=HIGHEST, preferred_element_type=F32)
            + jnp.dot(v2, e2_ref[...], precision=HIGHEST, preferred_element_type=F32))
    ecand = (_mxu(i1.astype(F32), e1_ref[...]) * float(n_keys) + _mxu(i2.astype(F32), e2_ref[...]))

    s_out = s_acc[...]
    e_out = e_acc[...]
    for r in range(PEER_TOPK):
        m = jnp.max(cand, axis=-1, keepdims=True)
        am = jnp.min(jnp.where(cand == m, lane2, 2 * LANES), axis=-1, keepdims=True)
        sel = lane2 == am
        ex = jnp.max(jnp.where(sel, ecand, -1.0), axis=-1, keepdims=True)
        cand = jnp.where(sel, -jnp.inf, cand)
        tgt = lane == h * PEER_TOPK + r
        s_out = jnp.where(tgt, m, s_out)
        e_out = jnp.where(tgt, ex, e_out)
        if r == 0:
            m_acc[...] = jnp.where((lane >> 4) == h, m, m_acc[...])
    s_acc[...] = s_out
    e_acc[...] = e_out

    @pl.when(h == pl.num_programs(1) - 1)
    def _():
        p = jnp.exp(s_out - m_acc[...])
        den = jnp.dot(p, blk_ref[...], precision=HIGHEST, preferred_element_type=F32)
        g_ref[...] = p / den
        e = e_out.astype(jnp.int32)
        row_ref[...] = e & ((1 << half_experts_log2) - 1)
        hi_ref[...] = e >> half_experts_log2


def _route_constants():
    lane = np.arange(2 * LANES)
    e1 = np.zeros((LANES, 2 * LANES), np.float32)
    e2 = np.zeros((LANES, 2 * LANES), np.float32)
    e1[lane // PEER_TOPK, lane] = 1.0
    e2[lane % PEER_TOPK, lane] = 1.0
    l1 = np.arange(LANES)
    blk = (l1[:, None] // PEER_TOPK == l1[None, :] // PEER_TOPK).astype(np.float32)
    return jnp.asarray(e1), jnp.asarray(e2), jnp.asarray(blk)


def _peer_route(q, keys_t, n_heads, half_experts_log2):
    t = q.shape[0]
    tb = PEER_ROUTE_TILE
    qd = q.shape[1] // n_heads
    n_keys = keys_t.shape[2]
    assert n_heads * PEER_TOPK == LANES and n_keys == LANES and qd == 2 * keys_t.shape[1]
    e1, e2, blk = _route_constants()
    full2 = lambda shape: pl.BlockSpec(shape, lambda i, h: (0,) * len(shape))
    out_spec = pl.BlockSpec((tb, LANES), lambda i, h: (i, 0))
    return pl.pallas_call(
        functools.partial(_route_kernel, n_keys=n_keys, half_experts_log2=half_experts_log2),
        grid=(t // tb, n_heads),
        in_specs=[pl.BlockSpec((tb, qd), lambda i, h: (i, h)), full2(keys_t.shape),
                  full2(e1.shape), full2(e2.shape), full2(blk.shape)],
        out_specs=[out_spec, out_spec, out_spec],
        out_shape=[jax.ShapeDtypeStruct((t, LANES), jnp.int32), jax.ShapeDtypeStruct((t, LANES), jnp.int32),
                   jax.ShapeDtypeStruct((t, LANES), F32)],
        scratch_shapes=[pltpu.VMEM((tb, LANES), F32)] * 3,
        compiler_params=_params(("parallel", "arbitrary")),
        name="peer_route",
    )(q, keys_t, e1, e2, blk)


_SLOT_OF_ROW = (0, 4, 2, 6, 1, 5, 3, 7)


def _sublane_sums(p, sub):
    lo4 = sub < 4
    c = []
    for a, b in ((p[0], p[1]), (p[2], p[3]), (p[4], p[5]), (p[6], p[7])):
        x = jnp.where(lo4, a, b)
        y = jnp.where(lo4, b, a)
        c.append(x + pltpu.roll(y, 4, axis=0))
    m2 = (sub & 2) == 0
    d = []
    for a, b in ((c[0], c[1]), (c[2], c[3])):
        d.append(jnp.where(m2, a + pltpu.roll(a, SUBLANES - 2, axis=0), b + pltpu.roll(b, 2, axis=0)))
    m1 = (sub & 1) == 0
    a, b = d
    return jnp.where(m1, a + pltpu.roll(a, SUBLANES - 1, axis=0), b + pltpu.roll(b, 1, axis=0))


def _peer_dot_kernel(idx_ref, x_ref, tbl_ref, o_ref, r_ref):
    tb = x_ref.shape[0]
    sub = lax.broadcasted_iota(jnp.int32, (SUBLANES, LANES), 0)

    def token(t, _):
        x = x_ref[t]
        for grp in range(LANES // SUBLANES):
            prods = [tbl_ref[idx_ref[t, grp * SUBLANES + _SLOT_OF_ROW[i]]] * x for i in range(SUBLANES)]
            r_ref[pl.ds(pl.multiple_of(t * LANES + grp * SUBLANES, SUBLANES), SUBLANES), :] = _sublane_sums(prods, sub)
        return 0

    lax.fori_loop(0, tb, token, 0)
    sums = jnp.dot(r_ref[...], jnp.ones((LANES, LANES), F32), precision=HIGHEST, preferred_element_type=F32)
    eye = (lax.broadcasted_iota(jnp.int32, (LANES, LANES), 0) == lax.broadcasted_iota(jnp.int32, (LANES, LANES), 1))
    o_ref[...] = jnp.sum(jnp.where(eye[None], sums.reshape(tb, LANES, LANES), 0.0), axis=1)


def _peer_dots(rows, x3, tbl):
    t = rows.shape[0]
    tb = PEER_TOKEN_TILE
    return pl.pallas_call(
        _peer_dot_kernel,
        grid=(t // tb,),
        in_specs=[pl.BlockSpec((tb, LANES), lambda i: (i, 0), memory_space=pltpu.SMEM),
                  pl.BlockSpec((tb, SUBLANES, LANES), lambda i: (i, 0, 0)),
                  pl.BlockSpec(memory_space=pltpu.VMEM)],
        out_specs=pl.BlockSpec((tb, LANES), lambda i: (i, 0)),
        out_shape=jax.ShapeDtypeStruct((t, LANES), F32),
        scratch_shapes=[pltpu.VMEM((tb * LANES, LANES), F32)],
        compiler_params=_params(("parallel",)),
        name="peer_dots",
    )(rows, x3, tbl)


def _peer_weight_kernel(d0_ref, d1_ref, hi_ref, g_ref, w0_ref, w1_ref):
    lo = hi_ref[...] == 0
    w = g_ref[...] * jax.nn.gelu(jnp.where(lo, d0_ref[...], d1_ref[...]))
    w0_ref[...] = jnp.where(lo, w, 0.0)
    w1_ref[...] = jnp.where(lo, 0.0, w)


def _peer_weights(d0, d1, hi, g):
    t = d0.shape[0]
    tm = TOKEN_TILE
    out = jax.ShapeDtypeStruct((t, LANES), F32)
    return pl.pallas_call(
        _peer_weight_kernel,
        grid=(t // tm,),
        in_specs=[_row_spec(tm, LANES)] * 4,
        out_specs=[_row_spec(tm, LANES)] * 2,
        out_shape=[out, out],
        compiler_params=_params(("parallel",)),
        name="peer_weights",
    )(d0, d1, hi, g)


def _peer_mix_kernel(idx_ref, w_ref, tbl_ref, o_ref):
    tb = o_ref.shape[0]
    n_acc = 4

    def token(t, _):
        acc = [jnp.zeros((SUBLANES, LANES), F32) for _ in range(n_acc)]
        for j in range(LANES):
            acc[j % n_acc] = acc[j % n_acc] + w_ref[t, j] * tbl_ref[idx_ref[t, j]]
        o_ref[t] = (acc[0] + acc[1]) + (acc[2] + acc[3])
        return 0

    lax.fori_loop(0, tb, token, 0)


def _peer_mix(rows, w, tbl):
    t = rows.shape[0]
    tb = PEER_TOKEN_TILE
    smem = pl.BlockSpec((tb, LANES), lambda i: (i, 0), memory_space=pltpu.SMEM)
    return pl.pallas_call(
        _peer_mix_kernel,
        grid=(t // tb,),
        in_specs=[smem, smem, pl.BlockSpec(memory_space=pltpu.VMEM)],
        out_specs=pl.BlockSpec((tb, SUBLANES, LANES), lambda i: (i, 0, 0)),
        out_shape=jax.ShapeDtypeStruct((t, SUBLANES, LANES), F32),
        compiler_params=_params(("parallel",)),
        name="peer_mix",
    )(rows, w, tbl)


def _add_ln_kernel(h_ref, f0_ref, f1_ref, g_ref, beta_ref, o_ref):
    y = DEEPNORM_ALPHA * h_ref[...] + (f0_ref[...] + f1_ref[...])
    o_ref[...] = _layer_norm(y, g_ref[...], beta_ref[...])


def _add_ln(h, f0, f1, g, beta):
    t, d = h.shape
    tm = TOKEN_TILE
    return pl.pallas_call(
        _add_ln_kernel,
        grid=(t // tm,),
        in_specs=[_row_spec(tm, d)] * 3 + [_full_spec((1, d))] * 2,
        out_specs=_row_spec(tm, d),
        out_shape=jax.ShapeDtypeStruct((t, d), F32),
        compiler_params=_params(("parallel",)),
        name="add_ln",
    )(h, f0, f1, g, beta)


def _peer_ffn(h, w_query, sub_keys, expert_u, expert_v, g, beta):
    t, d = h.shape
    n_experts = expert_u.shape[0]
    n_keys = sub_keys.shape[1]
    n_heads = w_query.shape[1] // (2 * sub_keys.shape[2])
    half = n_experts // 2
    assert d == SUBLANES * LANES and n_keys * n_keys == n_experts and half & (half - 1) == 0
    q = _matmul(h, w_query.astype(BF16))
    keys_t = jnp.swapaxes(sub_keys, 1, 2).astype(BF16)
    rows, hi, gate = _peer_route(q, keys_t, n_heads, half.bit_length() - 1)
    x3 = h.reshape(t, SUBLANES, LANES)
    u3 = expert_u.reshape(n_experts, SUBLANES, LANES)
    v3 = expert_v.reshape(n_experts, SUBLANES, LANES)
    d0 = _peer_dots(rows, x3, u3[:half])
    d1 = _peer_dots(rows, x3, u3[half:])
    w0, w1 = _peer_weights(d0, d1, hi, gate)
    f0 = _peer_mix(rows, w0, v3[:half]).reshape(t, d)
    f1 = _peer_mix(rows, w1, v3[half:]).reshape(t, d)
    return _add_ln(h, f0, f1, g, beta)


def _row(v):
    return v.reshape(1, -1)


def kernel(x, meta_tokens, lru_w_in, lru_conv_w, lru_conv_b, lru_w_a, lru_b_a, lru_w_x, lru_b_x, lru_lambda, lru_w_out, conf_w_pw1, conf_b_pw1, conf_dw_w, conf_dw_b, conf_ln_g, conf_ln_b, conf_w_pw2, conf_b_pw2, peer_w_query, peer_sub_keys, peer_u, peer_v, ln_mix_g, ln_mix_b, ln_ffn_g, ln_ffn_b):
    bsz, seq, d = x.shape
    s_tot = N_META_TOKENS + seq
    t = s_tot * bsz
    assert bsz % SUBLANES == 0 and t % TOKEN_TILE == 0 and s_tot % SCAN_STEPS == 0
    assert TOKEN_TILE % (2 * bsz) == 0 and t % PEER_ROUTE_TILE == 0
    meta = jnp.broadcast_to(meta_tokens.astype(x.dtype)[None], (bsz, N_META_TOKENS, d))
    h = jnp.concatenate([meta, x], axis=1)
    h = jnp.transpose(h, (1, 0, 2)).reshape(t, d)

    for i in range(DEPTH):
        j = i // 2
        if i % 2 == 0:
            gu = _matmul(h, lru_w_in[j].astype(BF16))
            a_f, b_f, a_r, b_r = _lru_coefs(
                gu, lru_conv_w[j], _row(lru_conv_b[j]), lru_w_a[j].astype(BF16), lru_b_a[j],
                lru_w_x[j].astype(BF16), lru_b_x[j], lru_lambda[j], bsz)
            h_f, h_r = _lru_scan(a_f, b_f, a_r, b_r, bsz)
            h = _lru_out(h_f, h_r, gu, lru_w_out[j].astype(BF16), h, _row(ln_mix_g[i]), _row(ln_mix_b[i]))
        else:
            hg = _matmul_glu(h, conf_w_pw1[j].astype(BF16), _row(conf_b_pw1[j]))
            hc = _dwconv_ln_silu(hg, conf_dw_w[j], _row(conf_dw_b[j]), _row(conf_ln_g[j]), _row(conf_ln_b[j]), bsz)
            h = _matmul_res_ln(hc, conf_w_pw2[j].astype(BF16), _row(conf_b_pw2[j]), h,
                               _row(ln_mix_g[i]), _row(ln_mix_b[i]))
        h = _peer_ffn(h, peer_w_query[i], peer_sub_keys[i], peer_u[i], peer_v[i],
                      _row(ln_ffn_g[i]), _row(ln_ffn_b[i]))
    out = h.reshape(s_tot, bsz, d)[N_META_TOKENS:]
    return jnp.transpose(out, (1, 0, 2))
```

```python
import functools

import jax
import jax.numpy as jnp
import numpy as np
from jax import lax
from jax.experimental import pallas as pl
from jax.experimental.pallas import tpu as pltpu

F32 = jnp.float32
BF16 = jnp.bfloat16
HIGHEST = lax.Precision.HIGHEST

N_META_TOKENS = 16
LRU_C = 8.0
LN_EPS = 1e-5
DEPTH = 2
DEEPNORM_ALPHA = (2.0 * DEPTH) ** 0.25

SUBLANES = 8
LANES = 128
VMEM_LIMIT_BYTES = 56 * 1024 * 1024

TOKEN_TILE = 512
SCAN_STEPS = 48
SCAN_CH = 256
PEER_TOKEN_TILE = 64
PEER_TOPK = 16


def _params(semantics):
    return pltpu.CompilerParams(dimension_semantics=semantics, vmem_limit_bytes=VMEM_LIMIT_BYTES)


def _mxu(a, b):
    return jnp.dot(a.astype(BF16), b.astype(BF16), preferred_element_type=F32)


def _layer_norm(y, g, b):
    mu = jnp.mean(y, axis=-1, keepdims=True)
    d = y - mu
    var = jnp.mean(d * d, axis=-1, keepdims=True)
    return d * lax.rsqrt(var + LN_EPS) * g + b


def _expm1(x):
    series = x * (1.0 + x * (1 / 2 + x * (1 / 6 + x * (1 / 24 + x * (1 / 120 + x * (1 / 720))))))
    return jnp.where(jnp.abs(x) < 0.1, series, jnp.exp(x) - 1.0)


def _row_spec(tm, n, col=0):
    return pl.BlockSpec((tm, n), lambda i: (i, col))


def _full_spec(shape):
    nd = len(shape)
    return pl.BlockSpec(shape, lambda i: (0,) * nd)


def _mm_kernel(x_ref, w_ref, o_ref):
    o_ref[...] = _mxu(x_ref[...], w_ref[...])


def _matmul(x, w):
    t, k = x.shape
    n = w.shape[1]
    return pl.pallas_call(
        _mm_kernel,
        grid=(t // TOKEN_TILE,),
        in_specs=[_row_spec(TOKEN_TILE, k), _full_spec((k, n))],
        out_specs=_row_spec(TOKEN_TILE, n),
        out_shape=jax.ShapeDtypeStruct((t, n), F32),
        compiler_params=_params(("parallel",)),
        name="matmul",
    )(x, w)


def _mm_glu_kernel(x_ref, w_ref, b_ref, o_ref):
    z = _mxu(x_ref[...], w_ref[...]) + b_ref[...]
    d = o_ref.shape[1]
    o_ref[...] = z[:, :d] * jax.nn.sigmoid(z[:, d:])


def _matmul_glu(x, w, b):
    t, k = x.shape
    n = w.shape[1]
    return pl.pallas_call(
        _mm_glu_kernel,
        grid=(t // TOKEN_TILE,),
        in_specs=[_row_spec(TOKEN_TILE, k), _full_spec((k, n)), _full_spec((1, n))],
        out_specs=_row_spec(TOKEN_TILE, n // 2),
        out_shape=jax.ShapeDtypeStruct((t, n // 2), F32),
        compiler_params=_params(("parallel",)),
        name="matmul_glu",
    )(x, w, b)


def _mm_res_ln_kernel(a_ref, w_ref, b_ref, res_ref, g_ref, beta_ref, o_ref):
    y = DEEPNORM_ALPHA * res_ref[...] + _mxu(a_ref[...], w_ref[...]) + b_ref[...]
    o_ref[...] = _layer_norm(y, g_ref[...], beta_ref[...])


def _matmul_res_ln(a, w, b, res, g, beta):
    t, k = a.shape
    n = w.shape[1]
    return pl.pallas_call(
        _mm_res_ln_kernel,
        grid=(t // TOKEN_TILE,),
        in_specs=[_row_spec(TOKEN_TILE, k), _full_spec((k, n)), _full_spec((1, n)),
                  _row_spec(TOKEN_TILE, n), _full_spec((1, n)), _full_spec((1, n))],
        out_specs=_row_spec(TOKEN_TILE, n),
        out_shape=jax.ShapeDtypeStruct((t, n), F32),
        compiler_params=_params(("parallel",)),
        name="matmul_res_ln",
    )(a, w, b, res, g, beta)


def _lru_coef_kernel(u_ref, prev_ref, next_ref, cw_ref, cb_ref, wa_ref, ba_ref, wx_ref, bx_ref,
                     lam_ref, af_ref, bf_ref, ar_ref, br_ref, *, batch):
    i = pl.program_id(0)
    last = pl.num_programs(0) - 1
    tm = u_ref.shape[0]
    prev = jnp.where(i > 0, prev_ref[...], 0.0)
    nxt = jnp.where(i < last, next_ref[...], 0.0)
    ext = jnp.concatenate([prev, u_ref[...], nxt], axis=0)
    uc = cb_ref[...] + sum(cw_ref[k:k + 1, :] * ext[k * batch:k * batch + tm, :] for k in range(4))
    n_blocks = wa_ref.shape[1]
    bw = wa_ref.shape[2]
    outs = ((af_ref, bf_ref), (ar_ref, br_ref))
    for n in range(n_blocks):
        sl = slice(n * bw, (n + 1) * bw)
        ub = uc[:, sl]
        ub16 = ub.astype(BF16)
        for d in range(2):
            r = jax.nn.sigmoid(jnp.dot(ub16, wa_ref[d, n], preferred_element_type=F32) + ba_ref[d:d + 1, sl])
            g = jax.nn.sigmoid(jnp.dot(ub16, wx_ref[d, n], preferred_element_type=F32) + bx_ref[d:d + 1, sl])
            log_a = (-LRU_C * r) * jax.nn.softplus(-lam_ref[d:d + 1, sl])
            a_ref, b_ref = outs[d]
            a_ref[:, sl] = jnp.exp(log_a)
            b_ref[:, sl] = jnp.sqrt(-_expm1(2.0 * log_a)) * (g * ub)


def _lru_coefs(gu, conv_w, conv_b, w_a, b_a, w_x, b_x, lam, batch):
    t = gu.shape[0]
    d = conv_w.shape[1]
    tm = TOKEN_TILE
    n_prev = tm // (2 * batch)
    n_next = tm // batch
    last_next = t // batch - 1
    out = jax.ShapeDtypeStruct((t, d), F32)
    return pl.pallas_call(
        functools.partial(_lru_coef_kernel, batch=batch),
        grid=(t // tm,),
        in_specs=[
            pl.BlockSpec((tm, d), lambda i: (i, 1)),
            pl.BlockSpec((2 * batch, d), lambda i: (jnp.maximum(i * n_prev - 1, 0), 1)),
            pl.BlockSpec((batch, d), lambda i: (jnp.minimum((i + 1) * n_next, last_next), 1)),
            _full_spec(conv_w.shape), _full_spec((1, d)),
            _full_spec(w_a.shape), _full_spec(b_a.shape),
            _full_spec(w_x.shape), _full_spec(b_x.shape), _full_spec(lam.shape),
        ],
        out_specs=[_row_spec(tm, d)] * 4,
        out_shape=[out] * 4,
        compiler_params=_params(("parallel",)),
        name="lru_coefs",
    )(gu, gu, gu, conv_w, conv_b, w_a, b_a, w_x, b_x, lam)


def _lru_scan_kernel(af_ref, bf_ref, ar_ref, br_ref, hf_ref, hr_ref, cf_ref, cr_ref, *, batch):
    j = pl.program_id(1)

    @pl.when(j == 0)
    def _():
        cf_ref[...] = jnp.zeros_like(cf_ref)
        cr_ref[...] = jnp.zeros_like(cr_ref)

    steps = af_ref.shape[0] // batch

    def body(s, carry):
        hf, hr = carry
        rf = pl.multiple_of(s * batch, batch)
        hf = af_ref[pl.ds(rf, batch), :] * hf + bf_ref[pl.ds(rf, batch), :]
        hf_ref[pl.ds(rf, batch), :] = hf
        rr = pl.multiple_of((steps - 1 - s) * batch, batch)
        hr = ar_ref[pl.ds(rr, batch), :] * hr + br_ref[pl.ds(rr, batch), :]
        hr_ref[pl.ds(rr, batch), :] = hr
        return hf, hr

    hf, hr = lax.fori_loop(0, steps, body, (cf_ref[...], cr_ref[...]), unroll=4)
    cf_ref[...] = hf
    cr_ref[...] = hr


def _lru_scan(a_f, b_f, a_r, b_r, batch):
    t, d = a_f.shape
    rows = SCAN_STEPS * batch
    n_chunks = t // rows
    fwd = pl.BlockSpec((rows, SCAN_CH), lambda c, j: (j, c))
    rev = pl.BlockSpec((rows, SCAN_CH), lambda c, j: (n_chunks - 1 - j, c))
    out = jax.ShapeDtypeStruct((t, d), F32)
    return pl.pallas_call(
        functools.partial(_lru_scan_kernel, batch=batch),
        grid=(d // SCAN_CH, n_chunks),
        in_specs=[fwd, fwd, rev, rev],
        out_specs=[fwd, rev],
        out_shape=[out, out],
        scratch_shapes=[pltpu.VMEM((batch, SCAN_CH), F32), pltpu.VMEM((batch, SCAN_CH), F32)],
        compiler_params=_params(("parallel", "arbitrary")),
        name="lru_scan",
    )(a_f, b_f, a_r, b_r)


def _lru_out_kernel(hf_ref, hr_ref, gate_ref, w_ref, res_ref, g_ref, beta_ref, o_ref):
    a = (hf_ref[...] + hr_ref[...]) * jax.nn.gelu(gate_ref[...])
    y = DEEPNORM_ALPHA * res_ref[...] + _mxu(a, w_ref[...])
    o_ref[...] = _layer_norm(y, g_ref[...], beta_ref[...])


def _lru_out(h_f, h_r, gu, w_out, res, g, beta):
    t, d = h_f.shape
    n = w_out.shape[1]
    tm = TOKEN_TILE
    return pl.pallas_call(
        _lru_out_kernel,
        grid=(t // tm,),
        in_specs=[_row_spec(tm, d), _row_spec(tm, d), _row_spec(tm, d, col=0), _full_spec((d, n)),
                  _row_spec(tm, n), _full_spec((1, n)), _full_spec((1, n))],
        out_specs=_row_spec(tm, n),
        out_shape=jax.ShapeDtypeStruct((t, n), F32),
        compiler_params=_params(("parallel",)),
        name="lru_out",
    )(h_f, h_r, gu, w_out, res, g, beta)


def _dwconv_kernel(cur_ref, prev_ref, next_ref, w_ref, b_ref, g_ref, beta_ref, o_ref, ext_ref, *, batch):
    i = pl.program_id(0)
    last = pl.num_programs(0) - 1
    tm = cur_ref.shape[0]
    taps = w_ref.shape[0]
    halo = (taps // 2) * batch
    ext_ref[0:halo, :] = jnp.where(i > 0, prev_ref[tm - halo:tm, :], 0.0)
    ext_ref[halo:halo + tm, :] = cur_ref[...]
    ext_ref[halo + tm:halo + tm + halo, :] = jnp.where(i < last, next_ref[0:halo, :], 0.0)

    def body(c, _):
        r0 = pl.multiple_of(c * batch, batch)
        acc = jnp.zeros((batch, cur_ref.shape[1]), F32) + b_ref[...]
        for k in range(taps):
            acc = acc + w_ref[k:k + 1, :] * ext_ref[pl.ds(r0 + k * batch, batch), :]
        y = _layer_norm(acc, g_ref[...], beta_ref[...])
        o_ref[pl.ds(r0, batch), :] = y * jax.nn.sigmoid(y)
        return 0

    lax.fori_loop(0, tm // batch, body, 0)


def _dwconv_ln_silu(x, w, b, g, beta, batch):
    t, d = x.shape
    tm = TOKEN_TILE
    n_tiles = t // tm
    halo = (w.shape[0] // 2) * batch
    assert halo <= tm
    return pl.pallas_call(
        functools.partial(_dwconv_kernel, batch=batch),
        grid=(n_tiles,),
        in_specs=[
            _row_spec(tm, d),
            pl.BlockSpec((tm, d), lambda i: (jnp.maximum(i - 1, 0), 0)),
            pl.BlockSpec((tm, d), lambda i: (jnp.minimum(i + 1, n_tiles - 1), 0)),
            _full_spec(w.shape), _full_spec((1, d)), _full_spec((1, d)), _full_spec((1, d)),
        ],
        out_specs=_row_spec(tm, d),
        out_shape=jax.ShapeDtypeStruct((t, d), F32),
        scratch_shapes=[pltpu.VMEM((tm + 2 * halo, d), F32)],
        compiler_params=_params(("parallel",)),
        name="dwconv_ln_silu",
    )(x, x, x, w, b, g, beta)


def _tree(tiles, op):
    tiles = list(tiles)
    while len(tiles) > 1:
        tiles = [op(tiles[i], tiles[i + 1]) for i in range(0, len(tiles) - 1, 2)] + tiles[len(tiles) & ~1:]
    return tiles[0]


def _all_rows(tiles, op):
    x = _tree(tiles, op)
    for shift in (4, 2, 1):
        x = op(x, pltpu.roll(x, shift, axis=0))
    return x


def _rows_of(tiles8, sub):
    out = tiles8[SUBLANES - 1]
    for i in range(SUBLANES - 2, -1, -1):
        out = jnp.where(sub == i, tiles8[i], out)
    return out


def _top16_rows(s, row_id):
    vals, ids = [], []
    not_found = float(len(s) * SUBLANES)
    for _ in range(PEER_TOPK):
        m = _all_rows(s, jnp.maximum)
        am = _all_rows([jnp.where(x == m, i, not_found) for x, i in zip(s, row_id)], jnp.minimum)
        s = [jnp.where(i == am, -jnp.inf, x) for x, i in zip(s, row_id)]
        vals.append(m)
        ids.append(am)
    return vals, ids


def _candidates(r1, r2, sub):
    lo4 = sub < 4
    r2_lo, r2_hi, r1_hi = _rows_of(r2[:8], sub), _rows_of(r2[8:], sub), _rows_of(r1[8:], sub)
    r2_dup4 = jnp.where(lo4, r2_lo, pltpu.roll(r2_lo, 4, axis=0))
    left = [r1[0], r1[0], r1[1], r1[2], r1[3], jnp.where(lo4, r1[4], r1[5]), jnp.where(lo4, r1[6], r1[7]), r1_hi]
    right = [r2_lo, r2_hi, r2_lo, r2_lo, r2_lo, r2_dup4, r2_dup4, r2[0]]
    return left, right


def _cand_flat(sub):
    subf = sub.astype(F32)
    lo4 = sub < 4
    k = float(PEER_TOPK)
    return [subf, 8.0 + subf, k + subf, 2 * k + subf, 3 * k + subf,
            jnp.where(lo4, 4 * k + subf, 5 * k + subf - 4.0), jnp.where(lo4, 6 * k + subf, 7 * k + subf - 4.0),
            (8.0 + subf) * k]


def _route_kernel(q_ref, keys_ref, off_ref, hi_ref, g_ref, *, n_keys, half_experts_log2):
    half = keys_ref.shape[2]
    n_heads = q_ref.shape[1] // (2 * half)
    sub = lax.broadcasted_iota(jnp.int32, (SUBLANES, LANES), 0)
    subf = sub.astype(F32)
    key_id = [subf + float(v * SUBLANES) for v in range(n_keys // SUBLANES)]
    flat = _cand_flat(sub)
    contract_last = (((1,), (1,)), ((), ()))
    for h in range(n_heads):
        tops = []
        for p in range(2):
            qp = q_ref[:, (2 * h + p) * half:(2 * h + p + 1) * half].astype(BF16)
            st = lax.dot_general(keys_ref[p], qp, contract_last, preferred_element_type=F32)
            tops.append(_top16_rows([st[v * SUBLANES:(v + 1) * SUBLANES, :] for v in range(n_keys // SUBLANES)], key_id))
        (v1, i1), (v2, i2) = tops
        cand = [a + b for a, b in zip(*_candidates(v1, v2, sub))]
        expert = [a * float(n_keys) + b for a, b in zip(*_candidates(i1, i2, sub))]
        s_rank, e_rank = [], []
        for _ in range(PEER_TOPK):
            m = _all_rows(cand, jnp.maximum)
            am = _all_rows([jnp.where(c == m, f, 256.0) for c, f in zip(cand, flat)], jnp.minimum)
            sel = [f == am for f in flat]
            e_rank.append(_all_rows([jnp.where(s, e, -1.0) for s, e in zip(sel, expert)], jnp.maximum))
            cand = [jnp.where(s, -jnp.inf, c) for s, c in zip(sel, cand)]
            s_rank.append(m)
        p = [jnp.exp(_rows_of(s_rank[r0:r0 + SUBLANES], sub) - s_rank[0]) for r0 in (0, SUBLANES)]
        den = _all_rows([p[0] + p[1]], jnp.add)
        for k, r0 in enumerate((0, SUBLANES)):
            rows = slice(h * PEER_TOPK + r0, h * PEER_TOPK + r0 + SUBLANES)
            g_ref[rows, :] = p[k] / den
            e = _rows_of(e_rank[r0:r0 + SUBLANES], sub).astype(jnp.int32)
            off_ref[rows, :] = (e & ((1 << half_experts_log2) - 1)) * SUBLANES
            hi_ref[rows, :] = e >> half_experts_log2


def _peer_route(q, keys, n_heads, half_experts_log2):
    t = q.shape[0]
    tb = LANES
    n_keys = keys.shape[1]
    assert n_heads * PEER_TOPK == LANES and n_keys == LANES and q.shape[1] == n_heads * 2 * keys.shape[2]
    assert PEER_TOPK == 2 * SUBLANES
    out_spec = pl.BlockSpec((LANES, tb), lambda i: (0, i))
    return pl.pallas_call(
        functools.partial(_route_kernel, n_keys=n_keys, half_experts_log2=half_experts_log2),
        grid=(t // tb,),
        in_specs=[_row_spec(tb, q.shape[1]), _full_spec(keys.shape)],
        out_specs=[out_spec, out_spec, out_spec],
        out_shape=[jax.ShapeDtypeStruct((LANES, t), jnp.int32), jax.ShapeDtypeStruct((LANES, t), jnp.int32),
                   jax.ShapeDtypeStruct((LANES, t), F32)],
        compiler_params=_params(("parallel",)),
        name="peer_route",
    )(q, keys)


_SLOT_OF_ROW = (0, 4, 2, 6, 1, 5, 3, 7)


def _sublane_sums(p, sub):
    lo4 = sub < 4
    c = []
    for a, b in ((p[0], p[1]), (p[2], p[3]), (p[4], p[5]), (p[6], p[7])):
        x = jnp.where(lo4, a, b)
        y = jnp.where(lo4, b, a)
        c.append(x + pltpu.roll(y, 4, axis=0))
    m2 = (sub & 2) == 0
    d = []
    for a, b in ((c[0], c[1]), (c[2], c[3])):
        d.append(jnp.where(m2, a + pltpu.roll(a, SUBLANES - 2, axis=0), b + pltpu.roll(b, 2, axis=0)))
    m1 = (sub & 1) == 0
    a, b = d
    return jnp.where(m1, a + pltpu.roll(a, SUBLANES - 1, axis=0), b + pltpu.roll(b, 1, axis=0))


def _lane_sums(r):
    hi = r.astype(BF16)
    lo = (r - hi.astype(F32)).astype(BF16)
    ones = jnp.ones((LANES, LANES), BF16)
    return jnp.dot(hi, ones, preferred_element_type=F32) + jnp.dot(lo, ones, preferred_element_type=F32)


def _table_row(tbl_ref, off):
    return tbl_ref[pl.ds(pl.multiple_of(off, SUBLANES), SUBLANES), :]


N_HEADS = LANES // PEER_TOPK


def _offsets_specs(tb):
    return [pl.BlockSpec((tb * N_HEADS,), lambda i: (i,), memory_space=pltpu.SMEM)] * PEER_TOPK


def _offsets_by_rank(offs_t):
    t = offs_t.shape[1]
    by_rank = jnp.transpose(offs_t.reshape(N_HEADS, PEER_TOPK, t), (1, 2, 0)).reshape(PEER_TOPK, t * N_HEADS)
    return [by_rank[r] for r in range(PEER_TOPK)]


def _peer_dot_kernel(*refs):
    off_ref = refs[:PEER_TOPK]
    x_ref, tbl_ref, o_ref, r_ref = refs[PEER_TOPK:]
    tb = x_ref.shape[0]
    sub = lax.broadcasted_iota(jnp.int32, (SUBLANES, LANES), 0)

    def token(t, _):
        x = x_ref[t]
        for h in range(LANES // PEER_TOPK):
            col = t * (LANES // PEER_TOPK) + h
            for r0 in range(0, PEER_TOPK, SUBLANES):
                prods = [_table_row(tbl_ref, off_ref[r0 + _SLOT_OF_ROW[i]][col]) * x for i in range(SUBLANES)]
                row = pl.multiple_of(t * LANES + h * PEER_TOPK + r0, SUBLANES)
                r_ref[pl.ds(row, SUBLANES), :] = _sublane_sums(prods, sub)
        return 0

    lax.fori_loop(0, tb, token, 0)
    eye = (lax.broadcasted_iota(jnp.int32, (LANES, LANES), 0) == lax.broadcasted_iota(jnp.int32, (LANES, LANES), 1))
    sums = _lane_sums(r_ref[...]).reshape(tb, LANES, LANES)
    o_ref[...] = jnp.sum(jnp.where(eye[None], sums, 0.0), axis=1)


def _peer_dots(offs, x3, tbl):
    tb = PEER_TOKEN_TILE
    t = x3.shape[0]
    return pl.pallas_call(
        _peer_dot_kernel,
        grid=(t // tb,),
        in_specs=_offsets_specs(tb) + [pl.BlockSpec((tb, SUBLANES, LANES), lambda i: (i, 0, 0)),
                                       pl.BlockSpec(memory_space=pltpu.VMEM)],
        out_specs=pl.BlockSpec((tb, LANES), lambda i: (i, 0)),
        out_shape=jax.ShapeDtypeStruct((t, LANES), F32),
        scratch_shapes=[pltpu.VMEM((tb * LANES, LANES), F32)],
        compiler_params=_params(("parallel",)),
        name="peer_dots",
    )(*offs, x3, tbl)


def _peer_weight_kernel(d0_ref, d1_ref, hi_ref, g_ref, w0_ref, w1_ref):
    lo = hi_ref[...] == 0
    w = g_ref[...] * jax.nn.gelu(jnp.where(lo, d0_ref[...], d1_ref[...]))
    w0_ref[...] = jnp.where(lo, w, 0.0)
    w1_ref[...] = jnp.where(lo, 0.0, w)


def _peer_weights(d0, d1, hi, g):
    t = d0.shape[0]
    tm = TOKEN_TILE
    out = jax.ShapeDtypeStruct((t, LANES), F32)
    return pl.pallas_call(
        _peer_weight_kernel,
        grid=(t // tm,),
        in_specs=[_row_spec(tm, LANES)] * 4,
        out_specs=[_row_spec(tm, LANES)] * 2,
        out_shape=[out, out],
        compiler_params=_params(("parallel",)),
        name="peer_weights",
    )(d0, d1, hi, g)


def _peer_mix_kernel(*refs):
    off_ref = refs[:PEER_TOPK]
    w_ref, tbl_ref, o_ref, wrep_ref = refs[PEER_TOPK:]
    tb = o_ref.shape[0]
    n_acc = 4
    eye = (lax.broadcasted_iota(jnp.int32, (LANES, LANES), 0) == lax.broadcasted_iota(jnp.int32, (LANES, LANES), 1))

    def spread(t, _):
        diag = jnp.where(eye, jnp.broadcast_to(w_ref[pl.ds(t, 1), :], (LANES, LANES)), 0.0)
        wrep_ref[pl.ds(pl.multiple_of(t * LANES, LANES), LANES), :] = _lane_sums(diag)
        return 0

    lax.fori_loop(0, tb, spread, 0, unroll=8)

    def token(t, _):
        base = pl.multiple_of(t * LANES, LANES)
        acc = [jnp.zeros((SUBLANES, LANES), F32) for _ in range(n_acc)]
        for h in range(LANES // PEER_TOPK):
            col = t * (LANES // PEER_TOPK) + h
            for r in range(PEER_TOPK):
                j = h * PEER_TOPK + r
                wv = jnp.broadcast_to(wrep_ref[pl.ds(base + j, 1), :], (SUBLANES, LANES))
                acc[j % n_acc] = acc[j % n_acc] + wv * _table_row(tbl_ref, off_ref[r][col])
        o_ref[t] = (acc[0] + acc[1]) + (acc[2] + acc[3])
        return 0

    lax.fori_loop(0, tb, token, 0)


def _peer_mix(offs, w, tbl):
    tb = PEER_TOKEN_TILE
    t = w.shape[0]
    return pl.pallas_call(
        _peer_mix_kernel,
        grid=(t // tb,),
        in_specs=_offsets_specs(tb) + [pl.BlockSpec((tb, LANES), lambda i: (i, 0)),
                                       pl.BlockSpec(memory_space=pltpu.VMEM)],
        out_specs=pl.BlockSpec((tb, SUBLANES, LANES), lambda i: (i, 0, 0)),
        out_shape=jax.ShapeDtypeStruct((t, SUBLANES, LANES), F32),
        scratch_shapes=[pltpu.VMEM((tb * LANES, LANES), F32)],
        compiler_params=_params(("parallel",)),
        name="peer_mix",
    )(*offs, w, tbl)


def _add_ln_kernel(h_ref, f0_ref, f1_ref, g_ref, beta_ref, o_ref):
    y = DEEPNORM_ALPHA * h_ref[...] + (f0_ref[...] + f1_ref[...])
    o_ref[...] = _layer_norm(y, g_ref[...], beta_ref[...])


def _add_ln(h, f0, f1, g, beta):
    t, d = h.shape
    tm = TOKEN_TILE
    return pl.pallas_call(
        _add_ln_kernel,
        grid=(t // tm,),
        in_specs=[_row_spec(tm, d)] * 3 + [_full_spec((1, d))] * 2,
        out_specs=_row_spec(tm, d),
        out_shape=jax.ShapeDtypeStruct((t, d), F32),
        compiler_params=_params(("parallel",)),
        name="add_ln",
    )(h, f0, f1, g, beta)


def _peer_ffn(h, w_query, sub_keys, expert_u, expert_v, g, beta):
    t, d = h.shape
    n_experts = expert_u.shape[0]
    n_keys = sub_keys.shape[1]
    n_heads = w_query.shape[1] // (2 * sub_keys.shape[2])
    half = n_experts // 2
    assert d == SUBLANES * LANES and n_keys * n_keys == n_experts and half & (half - 1) == 0
    q = _matmul(h, w_query.astype(BF16))
    offs_t, hi_t, gate_t = _peer_route(q, sub_keys.astype(BF16), n_heads, half.bit_length() - 1)
    hi, gate = jnp.transpose(hi_t), jnp.transpose(gate_t)
    x3 = h.reshape(t, SUBLANES, LANES)
    u2 = expert_u.reshape(n_experts * SUBLANES, LANES)
    v2 = expert_v.reshape(n_experts * SUBLANES, LANES)
    rows_half = half * SUBLANES
    off_blocks = _offsets_by_rank(offs_t)
    d0 = _peer_dots(off_blocks, x3, u2[:rows_half])
    d1 = _peer_dots(off_blocks, x3, u2[rows_half:])
    w0, w1 = _peer_weights(d0, d1, hi, gate)
    f0 = _peer_mix(off_blocks, w0, v2[:rows_half]).reshape(t, d)
    f1 = _peer_mix(off_blocks, w1, v2[rows_half:]).reshape(t, d)
    return _add_ln(h, f0, f1, g, beta)


def _row(v):
    return v.reshape(1, -1)


def kernel(x, meta_tokens, lru_w_in, lru_conv_w, lru_conv_b, lru_w_a, lru_b_a, lru_w_x, lru_b_x, lru_lambda, lru_w_out, conf_w_pw1, conf_b_pw1, conf_dw_w, conf_dw_b, conf_ln_g, conf_ln_b, conf_w_pw2, conf_b_pw2, peer_w_query, peer_sub_keys, peer_u, peer_v, ln_mix_g, ln_mix_b, ln_ffn_g, ln_ffn_b):
    bsz, seq, d = x.shape
    s_tot = N_META_TOKENS + seq
    t = s_tot * bsz
    assert bsz % SUBLANES == 0 and t % TOKEN_TILE == 0 and s_tot % SCAN_STEPS == 0
    assert TOKEN_TILE % (2 * bsz) == 0 and t % PEER_TOKEN_TILE == 0
    meta = jnp.broadcast_to(meta_tokens.astype(x.dtype)[None], (bsz, N_META_TOKENS, d))
    h = jnp.concatenate([meta, x], axis=1)
    h = jnp.transpose(h, (1, 0, 2)).reshape(t, d)

    for i in range(DEPTH):
        j = i // 2
        if i % 2 == 0:
            gu = _matmul(h, lru_w_in[j].astype(BF16))
            a_f, b_f, a_r, b_r = _lru_coefs(
                gu, lru_conv_w[j], _row(lru_conv_b[j]), lru_w_a[j].astype(BF16), lru_b_a[j],
                lru_w_x[j].astype(BF16), lru_b_x[j], lru_lambda[j], bsz)
            h_f, h_r = _lru_scan(a_f, b_f, a_r, b_r, bsz)
            h = _lru_out(h_f, h_r, gu, lru_w_out[j].astype(BF16), h, _row(ln_mix_g[i]), _row(ln_mix_b[i]))
        else:
            hg = _matmul_glu(h, conf_w_pw1[j].astype(BF16), _row(conf_b_pw1[j]))
            hc = _dwconv_ln_silu(hg, conf_dw_w[j], _row(conf_dw_b[j]), _row(conf_ln_g[j]), _row(conf_ln_b[j]), bsz)
            h = _matmul_res_ln(hc, conf_w_pw2[j].astype(BF16), _row(conf_b_pw2[j]), h,
                               _row(ln_mix_g[i]), _row(ln_mix_b[i]))
        h = _peer_ffn(h, peer_w_query[i], peer_sub_keys[i], peer_u[i], peer_v[i],
                      _row(ln_ffn_g[i]), _row(ln_ffn_b[i]))
    out = h.reshape(s_tot, bsz, d)[N_META_TOKENS:]
    return jnp.transpose(out, (1, 0, 2))
```

```python
import functools

import jax
import jax.numpy as jnp
import numpy as np
from jax import lax
from jax.experimental import pallas as pl
from jax.experimental.pallas import tpu as pltpu

F32 = jnp.float32
BF16 = jnp.bfloat16
HIGHEST = lax.Precision.HIGHEST

N_META_TOKENS = 16
LRU_C = 8.0
LN_EPS = 1e-5
DEPTH = 2
DEEPNORM_ALPHA = (2.0 * DEPTH) ** 0.25

SUBLANES = 8
LANES = 128
VMEM_LIMIT_BYTES = 56 * 1024 * 1024

TOKEN_TILE = 512
SCAN_STEPS = 48
SCAN_CH = 256
PEER_TOKEN_TILE = 128
PEER_TOPK = 16


def _params(semantics):
    return pltpu.CompilerParams(dimension_semantics=semantics, vmem_limit_bytes=VMEM_LIMIT_BYTES)


def _mxu(a, b):
    return jnp.dot(a.astype(BF16), b.astype(BF16), preferred_element_type=F32)


def _layer_norm(y, g, b):
    mu = jnp.mean(y, axis=-1, keepdims=True)
    d = y - mu
    var = jnp.mean(d * d, axis=-1, keepdims=True)
    return d * lax.rsqrt(var + LN_EPS) * g + b


def _expm1(x):
    series = x * (1.0 + x * (1 / 2 + x * (1 / 6 + x * (1 / 24 + x * (1 / 120 + x * (1 / 720))))))
    return jnp.where(jnp.abs(x) < 0.1, series, jnp.exp(x) - 1.0)


def _row_spec(tm, n, col=0):
    return pl.BlockSpec((tm, n), lambda i: (i, col))


def _full_spec(shape):
    nd = len(shape)
    return pl.BlockSpec(shape, lambda i: (0,) * nd)


def _mm_kernel(x_ref, w_ref, o_ref):
    o_ref[...] = _mxu(x_ref[...], w_ref[...])


def _matmul(x, w):
    t, k = x.shape
    n = w.shape[1]
    return pl.pallas_call(
        _mm_kernel,
        grid=(t // TOKEN_TILE,),
        in_specs=[_row_spec(TOKEN_TILE, k), _full_spec((k, n))],
        out_specs=_row_spec(TOKEN_TILE, n),
        out_shape=jax.ShapeDtypeStruct((t, n), F32),
        compiler_params=_params(("parallel",)),
        name="matmul",
    )(x, w)


def _mm_glu_kernel(x_ref, w_ref, b_ref, o_ref):
    z = _mxu(x_ref[...], w_ref[...]) + b_ref[...]
    d = o_ref.shape[1]
    o_ref[...] = z[:, :d] * jax.nn.sigmoid(z[:, d:])


def _matmul_glu(x, w, b):
    t, k = x.shape
    n = w.shape[1]
    return pl.pallas_call(
        _mm_glu_kernel,
        grid=(t // TOKEN_TILE,),
        in_specs=[_row_spec(TOKEN_TILE, k), _full_spec((k, n)), _full_spec((1, n))],
        out_specs=_row_spec(TOKEN_TILE, n // 2),
        out_shape=jax.ShapeDtypeStruct((t, n // 2), F32),
        compiler_params=_params(("parallel",)),
        name="matmul_glu",
    )(x, w, b)


def _mm_res_ln_kernel(a_ref, w_ref, b_ref, res_ref, g_ref, beta_ref, o_ref):
    y = DEEPNORM_ALPHA * res_ref[...] + _mxu(a_ref[...], w_ref[...]) + b_ref[...]
    o_ref[...] = _layer_norm(y, g_ref[...], beta_ref[...])


def _matmul_res_ln(a, w, b, res, g, beta):
    t, k = a.shape
    n = w.shape[1]
    return pl.pallas_call(
        _mm_res_ln_kernel,
        grid=(t // TOKEN_TILE,),
        in_specs=[_row_spec(TOKEN_TILE, k), _full_spec((k, n)), _full_spec((1, n)),
                  _row_spec(TOKEN_TILE, n), _full_spec((1, n)), _full_spec((1, n))],
        out_specs=_row_spec(TOKEN_TILE, n),
        out_shape=jax.ShapeDtypeStruct((t, n), F32),
        compiler_params=_params(("parallel",)),
        name="matmul_res_ln",
    )(a, w, b, res, g, beta)


def _lru_coef_kernel(u_ref, prev_ref, next_ref, cw_ref, cb_ref, wa_ref, ba_ref, wx_ref, bx_ref,
                     lam_ref, af_ref, bf_ref, ar_ref, br_ref, *, batch):
    i = pl.program_id(0)
    last = pl.num_programs(0) - 1
    tm = u_ref.shape[0]
    prev = jnp.where(i > 0, prev_ref[...], 0.0)
    nxt = jnp.where(i < last, next_ref[...], 0.0)
    ext = jnp.concatenate([prev, u_ref[...], nxt], axis=0)
    uc = cb_ref[...] + sum(cw_ref[k:k + 1, :] * ext[k * batch:k * batch + tm, :] for k in range(4))
    n_blocks = wa_ref.shape[1]
    bw = wa_ref.shape[2]
    outs = ((af_ref, bf_ref), (ar_ref, br_ref))
    for n in range(n_blocks):
        sl = slice(n * bw, (n + 1) * bw)
        ub = uc[:, sl]
        ub16 = ub.astype(BF16)
        for d in range(2):
            r = jax.nn.sigmoid(jnp.dot(ub16, wa_ref[d, n], preferred_element_type=F32) + ba_ref[d:d + 1, sl])
            g = jax.nn.sigmoid(jnp.dot(ub16, wx_ref[d, n], preferred_element_type=F32) + bx_ref[d:d + 1, sl])
            log_a = (-LRU_C * r) * jax.nn.softplus(-lam_ref[d:d + 1, sl])
            a_ref, b_ref = outs[d]
            a_ref[:, sl] = jnp.exp(log_a)
            b_ref[:, sl] = jnp.sqrt(-_expm1(2.0 * log_a)) * (g * ub)


def _lru_coefs(gu, conv_w, conv_b, w_a, b_a, w_x, b_x, lam, batch):
    t = gu.shape[0]
    d = conv_w.shape[1]
    tm = TOKEN_TILE
    n_prev = tm // (2 * batch)
    n_next = tm // batch
    last_next = t // batch - 1
    out = jax.ShapeDtypeStruct((t, d), F32)
    return pl.pallas_call(
        functools.partial(_lru_coef_kernel, batch=batch),
        grid=(t // tm,),
        in_specs=[
            pl.BlockSpec((tm, d), lambda i: (i, 1)),
            pl.BlockSpec((2 * batch, d), lambda i: (jnp.maximum(i * n_prev - 1, 0), 1)),
            pl.BlockSpec((batch, d), lambda i: (jnp.minimum((i + 1) * n_next, last_next), 1)),
            _full_spec(conv_w.shape), _full_spec((1, d)),
            _full_spec(w_a.shape), _full_spec(b_a.shape),
            _full_spec(w_x.shape), _full_spec(b_x.shape), _full_spec(lam.shape),
        ],
        out_specs=[_row_spec(tm, d)] * 4,
        out_shape=[out] * 4,
        compiler_params=_params(("parallel",)),
        name="lru_coefs",
    )(gu, gu, gu, conv_w, conv_b, w_a, b_a, w_x, b_x, lam)


def _lru_scan_kernel(af_ref, bf_ref, ar_ref, br_ref, hf_ref, hr_ref, cf_ref, cr_ref, *, batch):
    j = pl.program_id(1)

    @pl.when(j == 0)
    def _():
        cf_ref[...] = jnp.zeros_like(cf_ref)
        cr_ref[...] = jnp.zeros_like(cr_ref)

    steps = af_ref.shape[0] // batch

    def body(s, carry):
        hf, hr = carry
        rf = pl.multiple_of(s * batch, batch)
        hf = af_ref[pl.ds(rf, batch), :] * hf + bf_ref[pl.ds(rf, batch), :]
        hf_ref[pl.ds(rf, batch), :] = hf
        rr = pl.multiple_of((steps - 1 - s) * batch, batch)
        hr = ar_ref[pl.ds(rr, batch), :] * hr + br_ref[pl.ds(rr, batch), :]
        hr_ref[pl.ds(rr, batch), :] = hr
        return hf, hr

    hf, hr = lax.fori_loop(0, steps, body, (cf_ref[...], cr_ref[...]), unroll=4)
    cf_ref[...] = hf
    cr_ref[...] = hr


def _lru_scan(a_f, b_f, a_r, b_r, batch):
    t, d = a_f.shape
    rows = SCAN_STEPS * batch
    n_chunks = t // rows
    fwd = pl.BlockSpec((rows, SCAN_CH), lambda c, j: (j, c))
    rev = pl.BlockSpec((rows, SCAN_CH), lambda c, j: (n_chunks - 1 - j, c))
    out = jax.ShapeDtypeStruct((t, d), F32)
    return pl.pallas_call(
        functools.partial(_lru_scan_kernel, batch=batch),
        grid=(d // SCAN_CH, n_chunks),
        in_specs=[fwd, fwd, rev, rev],
        out_specs=[fwd, rev],
        out_shape=[out, out],
        scratch_shapes=[pltpu.VMEM((batch, SCAN_CH), F32), pltpu.VMEM((batch, SCAN_CH), F32)],
        compiler_params=_params(("parallel", "arbitrary")),
        name="lru_scan",
    )(a_f, b_f, a_r, b_r)


def _lru_out_kernel(hf_ref, hr_ref, gate_ref, w_ref, res_ref, g_ref, beta_ref, o_ref):
    a = (hf_ref[...] + hr_ref[...]) * jax.nn.gelu(gate_ref[...])
    y = DEEPNORM_ALPHA * res_ref[...] + _mxu(a, w_ref[...])
    o_ref[...] = _layer_norm(y, g_ref[...], beta_ref[...])


def _lru_out(h_f, h_r, gu, w_out, res, g, beta):
    t, d = h_f.shape
    n = w_out.shape[1]
    tm = TOKEN_TILE
    return pl.pallas_call(
        _lru_out_kernel,
        grid=(t // tm,),
        in_specs=[_row_spec(tm, d), _row_spec(tm, d), _row_spec(tm, d, col=0), _full_spec((d, n)),
                  _row_spec(tm, n), _full_spec((1, n)), _full_spec((1, n))],
        out_specs=_row_spec(tm, n),
        out_shape=jax.ShapeDtypeStruct((t, n), F32),
        compiler_params=_params(("parallel",)),
        name="lru_out",
    )(h_f, h_r, gu, w_out, res, g, beta)


def _dwconv_kernel(cur_ref, prev_ref, next_ref, w_ref, b_ref, g_ref, beta_ref, o_ref, ext_ref, *, batch):
    i = pl.program_id(0)
    last = pl.num_programs(0) - 1
    tm = cur_ref.shape[0]
    taps = w_ref.shape[0]
    halo = (taps // 2) * batch
    ext_ref[0:halo, :] = jnp.where(i > 0, prev_ref[tm - halo:tm, :], 0.0)
    ext_ref[halo:halo + tm, :] = cur_ref[...]
    ext_ref[halo + tm:halo + tm + halo, :] = jnp.where(i < last, next_ref[0:halo, :], 0.0)

    def body(c, _):
        r0 = pl.multiple_of(c * batch, batch)
        acc = jnp.zeros((batch, cur_ref.shape[1]), F32) + b_ref[...]
        for k in range(taps):
            acc = acc + w_ref[k:k + 1, :] * ext_ref[pl.ds(r0 + k * batch, batch), :]
        y = _layer_norm(acc, g_ref[...], beta_ref[...])
        o_ref[pl.ds(r0, batch), :] = y * jax.nn.sigmoid(y)
        return 0

    lax.fori_loop(0, tm // batch, body, 0)


def _dwconv_ln_silu(x, w, b, g, beta, batch):
    t, d = x.shape
    tm = TOKEN_TILE
    n_tiles = t // tm
    halo = (w.shape[0] // 2) * batch
    assert halo <= tm
    return pl.pallas_call(
        functools.partial(_dwconv_kernel, batch=batch),
        grid=(n_tiles,),
        in_specs=[
            _row_spec(tm, d),
            pl.BlockSpec((tm, d), lambda i: (jnp.maximum(i - 1, 0), 0)),
            pl.BlockSpec((tm, d), lambda i: (jnp.minimum(i + 1, n_tiles - 1), 0)),
            _full_spec(w.shape), _full_spec((1, d)), _full_spec((1, d)), _full_spec((1, d)),
        ],
        out_specs=_row_spec(tm, d),
        out_shape=jax.ShapeDtypeStruct((t, d), F32),
        scratch_shapes=[pltpu.VMEM((tm + 2 * halo, d), F32)],
        compiler_params=_params(("parallel",)),
        name="dwconv_ln_silu",
    )(x, x, x, w, b, g, beta)


def _tree(tiles, op):
    tiles = list(tiles)
    while len(tiles) > 1:
        tiles = [op(tiles[i], tiles[i + 1]) for i in range(0, len(tiles) - 1, 2)] + tiles[len(tiles) & ~1:]
    return tiles[0]


def _all_rows(tiles, op):
    x = _tree(tiles, op)
    for shift in (4, 2, 1):
        x = op(x, pltpu.roll(x, shift, axis=0))
    return x


def _rows_of(tiles8, sub):
    out = tiles8[SUBLANES - 1]
    for i in range(SUBLANES - 2, -1, -1):
        out = jnp.where(sub == i, tiles8[i], out)
    return out


def _top16_rows(s, row_id):
    vals, ids = [], []
    not_found = float(len(s) * SUBLANES)
    for _ in range(PEER_TOPK):
        m = _all_rows(s, jnp.maximum)
        am = _all_rows([jnp.where(x == m, i, not_found) for x, i in zip(s, row_id)], jnp.minimum)
        s = [jnp.where(i == am, -jnp.inf, x) for x, i in zip(s, row_id)]
        vals.append(m)
        ids.append(am)
    return vals, ids


def _candidates(r1, r2, sub):
    lo4 = sub < 4
    r2_lo, r2_hi, r1_hi = _rows_of(r2[:8], sub), _rows_of(r2[8:], sub), _rows_of(r1[8:], sub)
    r2_dup4 = jnp.where(lo4, r2_lo, pltpu.roll(r2_lo, 4, axis=0))
    left = [r1[0], r1[0], r1[1], r1[2], r1[3], jnp.where(lo4, r1[4], r1[5]), jnp.where(lo4, r1[6], r1[7]), r1_hi]
    right = [r2_lo, r2_hi, r2_lo, r2_lo, r2_lo, r2_dup4, r2_dup4, r2[0]]
    return left, right


def _cand_flat(sub):
    subf = sub.astype(F32)
    lo4 = sub < 4
    k = float(PEER_TOPK)
    return [subf, 8.0 + subf, k + subf, 2 * k + subf, 3 * k + subf,
            jnp.where(lo4, 4 * k + subf, 5 * k + subf - 4.0), jnp.where(lo4, 6 * k + subf, 7 * k + subf - 4.0),
            (8.0 + subf) * k]


def _route_kernel(q_ref, keys_ref, off_ref, gate_ref, cnt_ref, *, n_keys, half_experts_log2):
    half = keys_ref.shape[2]
    n_heads = q_ref.shape[1] // (2 * half)
    n_tiles = LANES // SUBLANES
    sub = lax.broadcasted_iota(jnp.int32, (SUBLANES, LANES), 0)
    subf = sub.astype(F32)
    key_id = [subf + float(v * SUBLANES) for v in range(n_keys // SUBLANES)]
    flat = _cand_flat(sub)
    contract_last = (((1,), (1,)), ((), ()))
    n_lo = jnp.zeros((SUBLANES, LANES), F32)
    n_hi = jnp.zeros((SUBLANES, LANES), F32)
    off_tiles = [jnp.zeros((SUBLANES, LANES), jnp.int32) for _ in range(n_tiles)]
    gate_tiles = [jnp.zeros((SUBLANES, LANES), F32) for _ in range(n_tiles)]
    for h in range(n_heads):
        tops = []
        for p in range(2):
            qp = q_ref[:, (2 * h + p) * half:(2 * h + p + 1) * half].astype(BF16)
            st = lax.dot_general(keys_ref[p], qp, contract_last, preferred_element_type=F32)
            tops.append(_top16_rows([st[v * SUBLANES:(v + 1) * SUBLANES, :] for v in range(n_keys // SUBLANES)], key_id))
        (v1, i1), (v2, i2) = tops
        cand = [a + b for a, b in zip(*_candidates(v1, v2, sub))]
        expert = [a * float(n_keys) + b for a, b in zip(*_candidates(i1, i2, sub))]
        s_rank, e_rank = [], []
        for _ in range(PEER_TOPK):
            m = _all_rows(cand, jnp.maximum)
            am = _all_rows([jnp.where(c == m, f, 256.0) for c, f in zip(cand, flat)], jnp.minimum)
            sel = [f == am for f in flat]
            e_rank.append(_all_rows([jnp.where(s, e, -1.0) for s, e in zip(sel, expert)], jnp.maximum))
            cand = [jnp.where(s, -jnp.inf, c) for s, c in zip(sel, cand)]
            s_rank.append(m)
        p = [jnp.exp(s - s_rank[0]) for s in s_rank]
        den = _tree(p, jnp.add)
        for r in range(PEER_TOPK):
            second_half = e_rank[r] >= float(1 << half_experts_log2)
            pos = jnp.where(second_half, float(LANES - 1) - n_hi, n_lo)
            n_hi = n_hi + jnp.where(second_half, 1.0, 0.0)
            n_lo = n_lo + jnp.where(second_half, 0.0, 1.0)
            e = e_rank[r].astype(jnp.int32)
            off = (e & ((1 << half_experts_log2) - 1)) * SUBLANES
            gate = p[r] / den
            gate = jnp.where(second_half, -gate, gate)
            for v in range(n_tiles):
                here = key_id[v] == pos
                off_tiles[v] = jnp.where(here, off, off_tiles[v])
                gate_tiles[v] = jnp.where(here, gate, gate_tiles[v])
    for v in range(n_tiles):
        rows = slice(v * SUBLANES, (v + 1) * SUBLANES)
        off_ref[rows, :] = off_tiles[v]
        gate_ref[rows, :] = gate_tiles[v]
    cnt_ref[...] = n_lo.astype(jnp.int32)


def _peer_route(q, keys, n_heads, half_experts_log2):
    t = q.shape[0]
    tb = LANES
    n_keys = keys.shape[1]
    assert n_heads * PEER_TOPK == LANES and n_keys == LANES and q.shape[1] == n_heads * 2 * keys.shape[2]
    out_spec = pl.BlockSpec((LANES, tb), lambda i: (0, i))
    return pl.pallas_call(
        functools.partial(_route_kernel, n_keys=n_keys, half_experts_log2=half_experts_log2),
        grid=(t // tb,),
        in_specs=[_row_spec(tb, q.shape[1]), _full_spec(keys.shape)],
        out_specs=[out_spec, out_spec, pl.BlockSpec((SUBLANES, tb), lambda i: (0, i))],
        out_shape=[jax.ShapeDtypeStruct((LANES, t), jnp.int32), jax.ShapeDtypeStruct((LANES, t), F32),
                   jax.ShapeDtypeStruct((SUBLANES, t), jnp.int32)],
        compiler_params=_params(("parallel",)),
        name="peer_route",
    )(q, keys)


_SLOT_OF_ROW = (0, 4, 2, 6, 1, 5, 3, 7)


def _sublane_sums(p, sub):
    lo4 = sub < 4
    c = []
    for a, b in ((p[0], p[1]), (p[2], p[3]), (p[4], p[5]), (p[6], p[7])):
        x = jnp.where(lo4, a, b)
        y = jnp.where(lo4, b, a)
        c.append(x + pltpu.roll(y, 4, axis=0))
    m2 = (sub & 2) == 0
    d = []
    for a, b in ((c[0], c[1]), (c[2], c[3])):
        d.append(jnp.where(m2, a + pltpu.roll(a, SUBLANES - 2, axis=0), b + pltpu.roll(b, 2, axis=0)))
    m1 = (sub & 1) == 0
    a, b = d
    return jnp.where(m1, a + pltpu.roll(a, SUBLANES - 1, axis=0), b + pltpu.roll(b, 1, axis=0))


def _lane_sums(r):
    hi = r.astype(BF16)
    lo = (r - hi.astype(F32)).astype(BF16)
    ones = jnp.ones((LANES, LANES), BF16)
    return jnp.dot(hi, ones, preferred_element_type=F32) + jnp.dot(lo, ones, preferred_element_type=F32)


def _table_row(tbl_ref, off):
    return tbl_ref[pl.ds(pl.multiple_of(off, SUBLANES), SUBLANES), :]


N_POS_TILES = LANES // SUBLANES
WINDOW_TILES = 9
GATHER_CHUNK = 32


def _gather_specs(tb):
    offs = [pl.BlockSpec((SUBLANES, tb), lambda i, k=k: (k, i), memory_space=pltpu.SMEM) for k in range(N_POS_TILES)]
    return offs + [pl.BlockSpec((SUBLANES, tb), lambda i: (0, i), memory_space=pltpu.SMEM)]


def _window(second_half):
    tiles = list(range(N_POS_TILES))
    if second_half:
        return tiles[N_POS_TILES - WINDOW_TILES:], tiles[:N_POS_TILES - WINDOW_TILES]
    return tiles[:WINDOW_TILES], tiles[WINDOW_TILES:]


def _needs_rest(cnt_ref, t, second_half):
    n_first = cnt_ref[0, t]
    n_mine = LANES - n_first if second_half else n_first
    return n_mine > WINDOW_TILES * SUBLANES


def _peer_dot_kernel(*refs, second_half):
    off_ref = refs[:N_POS_TILES]
    cnt_ref, x_ref, tbl_ref, o_ref, r_ref = refs[N_POS_TILES:]
    tb = x_ref.shape[0]
    sub = lax.broadcasted_iota(jnp.int32, (SUBLANES, LANES), 0)
    eye = (lax.broadcasted_iota(jnp.int32, (LANES, LANES), 0) == lax.broadcasted_iota(jnp.int32, (LANES, LANES), 1))
    window, rest = _window(second_half)

    def chunk(c, _):
        t0 = pl.multiple_of(c * GATHER_CHUNK, GATHER_CHUNK)

        def token(tl, _):
            t = t0 + tl
            x = x_ref[t]

            def rows(k):
                return pl.ds(pl.multiple_of(tl * LANES + k * SUBLANES, SUBLANES), SUBLANES)

            def tile(k):
                prods = [_table_row(tbl_ref, off_ref[k][_SLOT_OF_ROW[i], t]) * x for i in range(SUBLANES)]
                return _sublane_sums(prods, sub)

            for k in window:
                r_ref[rows(k), :] = tile(k)
            more = _needs_rest(cnt_ref, t, second_half)

            @pl.when(more)
            def _():
                for k in rest:
                    r_ref[rows(k), :] = tile(k)

            @pl.when(jnp.logical_not(more))
            def _():
                for k in rest:
                    r_ref[rows(k), :] = jnp.zeros((SUBLANES, LANES), F32)

            return 0

        lax.fori_loop(0, GATHER_CHUNK, token, 0)
        sums = _lane_sums(r_ref[...]).reshape(GATHER_CHUNK, LANES, LANES)
        o_ref[pl.ds(t0, GATHER_CHUNK), :] = jnp.sum(jnp.where(eye[None], sums, 0.0), axis=1)
        return 0

    lax.fori_loop(0, tb // GATHER_CHUNK, chunk, 0)


def _peer_dots(offs_t, count, x3, tbl, second_half):
    tb = PEER_TOKEN_TILE
    t = x3.shape[0]
    return pl.pallas_call(
        functools.partial(_peer_dot_kernel, second_half=second_half),
        grid=(t // tb,),
        in_specs=_gather_specs(tb) + [pl.BlockSpec((tb, SUBLANES, LANES), lambda i: (i, 0, 0)),
                                      pl.BlockSpec(memory_space=pltpu.VMEM)],
        out_specs=pl.BlockSpec((tb, LANES), lambda i: (i, 0)),
        out_shape=jax.ShapeDtypeStruct((t, LANES), F32),
        scratch_shapes=[pltpu.VMEM((GATHER_CHUNK * LANES, LANES), F32)],
        compiler_params=_params(("parallel",)),
        name="peer_dots",
    )(*([offs_t] * N_POS_TILES), count, x3, tbl)


def _peer_weight_kernel(d0_ref, d1_ref, gate_ref, w0_ref, w1_ref):
    first = gate_ref[...] > 0.0
    w = jnp.abs(gate_ref[...]) * jax.nn.gelu(jnp.where(first, d0_ref[...], d1_ref[...]))
    w0_ref[...] = jnp.where(first, w, 0.0)
    w1_ref[...] = jnp.where(first, 0.0, w)


def _peer_weights(d0, d1, gates):
    t = d0.shape[0]
    tm = TOKEN_TILE
    out = jax.ShapeDtypeStruct((t, LANES), F32)
    return pl.pallas_call(
        _peer_weight_kernel,
        grid=(t // tm,),
        in_specs=[_row_spec(tm, LANES)] * 3,
        out_specs=[_row_spec(tm, LANES)] * 2,
        out_shape=[out, out],
        compiler_params=_params(("parallel",)),
        name="peer_weights",
    )(d0, d1, gates)


def _peer_mix_kernel(*refs, second_half):
    off_ref = refs[:N_POS_TILES]
    cnt_ref, w_ref, tbl_ref, o_ref, wrep_ref = refs[N_POS_TILES:]
    tb = o_ref.shape[0]
    n_acc = 4
    eye = (lax.broadcasted_iota(jnp.int32, (LANES, LANES), 0) == lax.broadcasted_iota(jnp.int32, (LANES, LANES), 1))
    window, rest = _window(second_half)

    def chunk(c, _):
        t0 = pl.multiple_of(c * GATHER_CHUNK, GATHER_CHUNK)

        def spread(tl, _):
            diag = jnp.where(eye, jnp.broadcast_to(w_ref[pl.ds(t0 + tl, 1), :], (LANES, LANES)), 0.0)
            wrep_ref[pl.ds(pl.multiple_of(tl * LANES, LANES), LANES), :] = _lane_sums(diag)
            return 0

        lax.fori_loop(0, GATHER_CHUNK, spread, 0, unroll=8)

        def token(tl, _):
            t = t0 + tl
            base = pl.multiple_of(tl * LANES, LANES)

            def weighted_rows(tiles):
                acc = [jnp.zeros((SUBLANES, LANES), F32) for _ in range(n_acc)]
                for k in tiles:
                    for s in range(SUBLANES):
                        p = k * SUBLANES + s
                        wv = jnp.broadcast_to(wrep_ref[pl.ds(base + p, 1), :], (SUBLANES, LANES))
                        acc[p % n_acc] = acc[p % n_acc] + wv * _table_row(tbl_ref, off_ref[k][s, t])
                return (acc[0] + acc[1]) + (acc[2] + acc[3])

            o_ref[t] = weighted_rows(window)

            @pl.when(_needs_rest(cnt_ref, t, second_half))
            def _():
                o_ref[t] = o_ref[t] + weighted_rows(rest)

            return 0

        lax.fori_loop(0, GATHER_CHUNK, token, 0)
        return 0

    lax.fori_loop(0, tb // GATHER_CHUNK, chunk, 0)


def _peer_mix(offs_t, count, w, tbl, second_half):
    tb = PEER_TOKEN_TILE
    t = w.shape[0]
    return pl.pallas_call(
        functools.partial(_peer_mix_kernel, second_half=second_half),
        grid=(t // tb,),
        in_specs=_gather_specs(tb) + [pl.BlockSpec((tb, LANES), lambda i: (i, 0)),
                                      pl.BlockSpec(memory_space=pltpu.VMEM)],
        out_specs=pl.BlockSpec((tb, SUBLANES, LANES), lambda i: (i, 0, 0)),
        out_shape=jax.ShapeDtypeStruct((t, SUBLANES, LANES), F32),
        scratch_shapes=[pltpu.VMEM((GATHER_CHUNK * LANES, LANES), F32)],
        compiler_params=_params(("parallel",)),
        name="peer_mix",
    )(*([offs_t] * N_POS_TILES), count, w, tbl)


def _add_ln_kernel(h_ref, f0_ref, f1_ref, g_ref, beta_ref, o_ref):
    y = DEEPNORM_ALPHA * h_ref[...] + (f0_ref[...] + f1_ref[...])
    o_ref[...] = _layer_norm(y, g_ref[...], beta_ref[...])


def _add_ln(h, f0, f1, g, beta):
    t, d = h.shape
    tm = TOKEN_TILE
    return pl.pallas_call(
        _add_ln_kernel,
        grid=(t // tm,),
        in_specs=[_row_spec(tm, d)] * 3 + [_full_spec((1, d))] * 2,
        out_specs=_row_spec(tm, d),
        out_shape=jax.ShapeDtypeStruct((t, d), F32),
        compiler_params=_params(("parallel",)),
        name="add_ln",
    )(h, f0, f1, g, beta)


def _peer_ffn(h, w_query, sub_keys, expert_u, expert_v, g, beta):
    t, d = h.shape
    n_experts = expert_u.shape[0]
    n_keys = sub_keys.shape[1]
    n_heads = w_query.shape[1] // (2 * sub_keys.shape[2])
    half = n_experts // 2
    assert d == SUBLANES * LANES and n_keys * n_keys == n_experts and half & (half - 1) == 0
    q = _matmul(h, w_query.astype(BF16))
    offs_t, gates_t, count = _peer_route(q, sub_keys.astype(BF16), n_heads, half.bit_length() - 1)
    gates = jnp.transpose(gates_t)
    x3 = h.reshape(t, SUBLANES, LANES)
    u2 = expert_u.reshape(n_experts * SUBLANES, LANES)
    v2 = expert_v.reshape(n_experts * SUBLANES, LANES)
    rows_half = half * SUBLANES
    d0 = _peer_dots(offs_t, count, x3, u2[:rows_half], False)
    d1 = _peer_dots(offs_t, count, x3, u2[rows_half:], True)
    w0, w1 = _peer_weights(d0, d1, gates)
    f0 = _peer_mix(offs_t, count, w0, v2[:rows_half], False).reshape(t, d)
    f1 = _peer_mix(offs_t, count, w1, v2[rows_half:], True).reshape(t, d)
    return _add_ln(h, f0, f1, g, beta)


def _row(v):
    return v.reshape(1, -1)


def kernel(x, meta_tokens, lru_w_in, lru_conv_w, lru_conv_b, lru_w_a, lru_b_a, lru_w_x, lru_b_x, lru_lambda, lru_w_out, conf_w_pw1, conf_b_pw1, conf_dw_w, conf_dw_b, conf_ln_g, conf_ln_b, conf_w_pw2, conf_b_pw2, peer_w_query, peer_sub_keys, peer_u, peer_v, ln_mix_g, ln_mix_b, ln_ffn_g, ln_ffn_b):
    bsz, seq, d = x.shape
    s_tot = N_META_TOKENS + seq
    t = s_tot * bsz
    assert bsz % SUBLANES == 0 and t % TOKEN_TILE == 0 and s_tot % SCAN_STEPS == 0
    assert TOKEN_TILE % (2 * bsz) == 0 and t % PEER_TOKEN_TILE == 0
    meta = jnp.broadcast_to(meta_tokens.astype(x.dtype)[None], (bsz, N_META_TOKENS, d))
    h = jnp.concatenate([meta, x], axis=1)
    h = jnp.transpose(h, (1, 0, 2)).reshape(t, d)

    for i in range(DEPTH):
        j = i // 2
        if i % 2 == 0:
            gu = _matmul(h, lru_w_in[j].astype(BF16))
            a_f, b_f, a_r, b_r = _lru_coefs(
                gu, lru_conv_w[j], _row(lru_conv_b[j]), lru_w_a[j].astype(BF16), lru_b_a[j],
                lru_w_x[j].astype(BF16), lru_b_x[j], lru_lambda[j], bsz)
            h_f, h_r = _lru_scan(a_f, b_f, a_r, b_r, bsz)
            h = _lru_out(h_f, h_r, gu, lru_w_out[j].astype(BF16), h, _row(ln_mix_g[i]), _row(ln_mix_b[i]))
        else:
            hg = _matmul_glu(h, conf_w_pw1[j].astype(BF16), _row(conf_b_pw1[j]))
            hc = _dwconv_ln_silu(hg, conf_dw_w[j], _row(conf_dw_b[j]), _row(conf_ln_g[j]), _row(conf_ln_b[j]), bsz)
            h = _matmul_res_ln(hc, conf_w_pw2[j].astype(BF16), _row(conf_b_pw2[j]), h,
                               _row(ln_mix_g[i]), _row(ln_mix_b[i]))
        h = _peer_ffn(h, peer_w_query[i], peer_sub_keys[i], peer_u[i], peer_v[i],
                      _row(ln_ffn_g[i]), _row(ln_ffn_b[i]))
    out = h.reshape(s_tot, bsz, d)[N_META_TOKENS:]
    return jnp.transpose(out, (1, 0, 2))
```

```python
import functools

import jax
import jax.numpy as jnp
import numpy as np
from jax import lax
from jax.experimental import pallas as pl
from jax.experimental.pallas import tpu as pltpu

F32 = jnp.float32
BF16 = jnp.bfloat16
HIGHEST = lax.Precision.HIGHEST

N_META_TOKENS = 16
LRU_C = 8.0
LN_EPS = 1e-5
DEPTH = 2
DEEPNORM_ALPHA = (2.0 * DEPTH) ** 0.25

SUBLANES = 8
LANES = 128
VMEM_LIMIT_BYTES = 56 * 1024 * 1024

TOKEN_TILE = 512
SCAN_STEPS = 48
SCAN_CH = 256
PEER_TOKEN_TILE = 128
PEER_TOPK = 16


def _params(semantics):
    return pltpu.CompilerParams(dimension_semantics=semantics, vmem_limit_bytes=VMEM_LIMIT_BYTES)


def _mxu(a, b):
    return jnp.dot(a.astype(BF16), b.astype(BF16), preferred_element_type=F32)


def _layer_norm(y, g, b):
    mu = jnp.mean(y, axis=-1, keepdims=True)
    d = y - mu
    var = jnp.mean(d * d, axis=-1, keepdims=True)
    return d * lax.rsqrt(var + LN_EPS) * g + b


def _expm1(x):
    series = x * (1.0 + x * (1 / 2 + x * (1 / 6 + x * (1 / 24 + x * (1 / 120 + x * (1 / 720))))))
    return jnp.where(jnp.abs(x) < 0.1, series, jnp.exp(x) - 1.0)


def _row_spec(tm, n, col=0):
    return pl.BlockSpec((tm, n), lambda i: (i, col))


def _full_spec(shape):
    nd = len(shape)
    return pl.BlockSpec(shape, lambda i: (0,) * nd)


def _mm_kernel(x_ref, w_ref, o_ref):
    o_ref[...] = _mxu(x_ref[...], w_ref[...])


def _matmul(x, w):
    t, k = x.shape
    n = w.shape[1]
    return pl.pallas_call(
        _mm_kernel,
        grid=(t // TOKEN_TILE,),
        in_specs=[_row_spec(TOKEN_TILE, k), _full_spec((k, n))],
        out_specs=_row_spec(TOKEN_TILE, n),
        out_shape=jax.ShapeDtypeStruct((t, n), F32),
        compiler_params=_params(("parallel",)),
        name="matmul",
    )(x, w)


def _mm_glu_kernel(x_ref, w_ref, b_ref, o_ref):
    z = _mxu(x_ref[...], w_ref[...]) + b_ref[...]
    d = o_ref.shape[1]
    o_ref[...] = z[:, :d] * jax.nn.sigmoid(z[:, d:])


def _matmul_glu(x, w, b):
    t, k = x.shape
    n = w.shape[1]
    return pl.pallas_call(
        _mm_glu_kernel,
        grid=(t // TOKEN_TILE,),
        in_specs=[_row_spec(TOKEN_TILE, k), _full_spec((k, n)), _full_spec((1, n))],
        out_specs=_row_spec(TOKEN_TILE, n // 2),
        out_shape=jax.ShapeDtypeStruct((t, n // 2), F32),
        compiler_params=_params(("parallel",)),
        name="matmul_glu",
    )(x, w, b)


def _mm_res_ln_kernel(a_ref, w_ref, b_ref, res_ref, g_ref, beta_ref, o_ref):
    y = DEEPNORM_ALPHA * res_ref[...] + _mxu(a_ref[...], w_ref[...]) + b_ref[...]
    o_ref[...] = _layer_norm(y, g_ref[...], beta_ref[...])


def _matmul_res_ln(a, w, b, res, g, beta):
    t, k = a.shape
    n = w.shape[1]
    return pl.pallas_call(
        _mm_res_ln_kernel,
        grid=(t // TOKEN_TILE,),
        in_specs=[_row_spec(TOKEN_TILE, k), _full_spec((k, n)), _full_spec((1, n)),
                  _row_spec(TOKEN_TILE, n), _full_spec((1, n)), _full_spec((1, n))],
        out_specs=_row_spec(TOKEN_TILE, n),
        out_shape=jax.ShapeDtypeStruct((t, n), F32),
        compiler_params=_params(("parallel",)),
        name="matmul_res_ln",
    )(a, w, b, res, g, beta)


def _lru_coef_kernel(u_ref, prev_ref, next_ref, cw_ref, cb_ref, wa_ref, ba_ref, wx_ref, bx_ref,
                     lam_ref, af_ref, bf_ref, ar_ref, br_ref, *, batch):
    i = pl.program_id(0)
    last = pl.num_programs(0) - 1
    tm = u_ref.shape[0]
    prev = jnp.where(i > 0, prev_ref[...], 0.0)
    nxt = jnp.where(i < last, next_ref[...], 0.0)
    ext = jnp.concatenate([prev, u_ref[...], nxt], axis=0)
    uc = cb_ref[...] + sum(cw_ref[k:k + 1, :] * ext[k * batch:k * batch + tm, :] for k in range(4))
    n_blocks = wa_ref.shape[1]
    bw = wa_ref.shape[2]
    outs = ((af_ref, bf_ref), (ar_ref, br_ref))
    for n in range(n_blocks):
        sl = slice(n * bw, (n + 1) * bw)
        ub = uc[:, sl]
        ub16 = ub.astype(BF16)
        for d in range(2):
            r = jax.nn.sigmoid(jnp.dot(ub16, wa_ref[d, n], preferred_element_type=F32) + ba_ref[d:d + 1, sl])
            g = jax.nn.sigmoid(jnp.dot(ub16, wx_ref[d, n], preferred_element_type=F32) + bx_ref[d:d + 1, sl])
            log_a = (-LRU_C * r) * jax.nn.softplus(-lam_ref[d:d + 1, sl])
            a_ref, b_ref = outs[d]
            a_ref[:, sl] = jnp.exp(log_a)
            b_ref[:, sl] = jnp.sqrt(-_expm1(2.0 * log_a)) * (g * ub)


def _lru_coefs(gu, conv_w, conv_b, w_a, b_a, w_x, b_x, lam, batch):
    t = gu.shape[0]
    d = conv_w.shape[1]
    tm = TOKEN_TILE
    n_prev = tm // (2 * batch)
    n_next = tm // batch
    last_next = t // batch - 1
    out = jax.ShapeDtypeStruct((t, d), F32)
    return pl.pallas_call(
        functools.partial(_lru_coef_kernel, batch=batch),
        grid=(t // tm,),
        in_specs=[
            pl.BlockSpec((tm, d), lambda i: (i, 1)),
            pl.BlockSpec((2 * batch, d), lambda i: (jnp.maximum(i * n_prev - 1, 0), 1)),
            pl.BlockSpec((batch, d), lambda i: (jnp.minimum((i + 1) * n_next, last_next), 1)),
            _full_spec(conv_w.shape), _full_spec((1, d)),
            _full_spec(w_a.shape), _full_spec(b_a.shape),
            _full_spec(w_x.shape), _full_spec(b_x.shape), _full_spec(lam.shape),
        ],
        out_specs=[_row_spec(tm, d)] * 4,
        out_shape=[out] * 4,
        compiler_params=_params(("parallel",)),
        name="lru_coefs",
    )(gu, gu, gu, conv_w, conv_b, w_a, b_a, w_x, b_x, lam)


def _lru_scan_kernel(af_ref, bf_ref, ar_ref, br_ref, hf_ref, hr_ref, cf_ref, cr_ref, *, batch):
    j = pl.program_id(1)

    @pl.when(j == 0)
    def _():
        cf_ref[...] = jnp.zeros_like(cf_ref)
        cr_ref[...] = jnp.zeros_like(cr_ref)

    steps = af_ref.shape[0] // batch

    def body(s, carry):
        hf, hr = carry
        rf = pl.multiple_of(s * batch, batch)
        hf = af_ref[pl.ds(rf, batch), :] * hf + bf_ref[pl.ds(rf, batch), :]
        hf_ref[pl.ds(rf, batch), :] = hf
        rr = pl.multiple_of((steps - 1 - s) * batch, batch)
        hr = ar_ref[pl.ds(rr, batch), :] * hr + br_ref[pl.ds(rr, batch), :]
        hr_ref[pl.ds(rr, batch), :] = hr
        return hf, hr

    hf, hr = lax.fori_loop(0, steps, body, (cf_ref[...], cr_ref[...]), unroll=4)
    cf_ref[...] = hf
    cr_ref[...] = hr


def _lru_scan(a_f, b_f, a_r, b_r, batch):
    t, d = a_f.shape
    rows = SCAN_STEPS * batch
    n_chunks = t // rows
    fwd = pl.BlockSpec((rows, SCAN_CH), lambda c, j: (j, c))
    rev = pl.BlockSpec((rows, SCAN_CH), lambda c, j: (n_chunks - 1 - j, c))
    out = jax.ShapeDtypeStruct((t, d), F32)
    return pl.pallas_call(
        functools.partial(_lru_scan_kernel, batch=batch),
        grid=(d // SCAN_CH, n_chunks),
        in_specs=[fwd, fwd, rev, rev],
        out_specs=[fwd, rev],
        out_shape=[out, out],
        scratch_shapes=[pltpu.VMEM((batch, SCAN_CH), F32), pltpu.VMEM((batch, SCAN_CH), F32)],
        compiler_params=_params(("parallel", "arbitrary")),
        name="lru_scan",
    )(a_f, b_f, a_r, b_r)


def _lru_out_kernel(hf_ref, hr_ref, gate_ref, w_ref, res_ref, g_ref, beta_ref, o_ref):
    a = (hf_ref[...] + hr_ref[...]) * jax.nn.gelu(gate_ref[...])
    y = DEEPNORM_ALPHA * res_ref[...] + _mxu(a, w_ref[...])
    o_ref[...] = _layer_norm(y, g_ref[...], beta_ref[...])


def _lru_out(h_f, h_r, gu, w_out, res, g, beta):
    t, d = h_f.shape
    n = w_out.shape[1]
    tm = TOKEN_TILE
    return pl.pallas_call(
        _lru_out_kernel,
        grid=(t // tm,),
        in_specs=[_row_spec(tm, d), _row_spec(tm, d), _row_spec(tm, d, col=0), _full_spec((d, n)),
                  _row_spec(tm, n), _full_spec((1, n)), _full_spec((1, n))],
        out_specs=_row_spec(tm, n),
        out_shape=jax.ShapeDtypeStruct((t, n), F32),
        compiler_params=_params(("parallel",)),
        name="lru_out",
    )(h_f, h_r, gu, w_out, res, g, beta)


def _dwconv_kernel(cur_ref, prev_ref, next_ref, w_ref, b_ref, g_ref, beta_ref, o_ref, ext_ref, *, batch):
    i = pl.program_id(0)
    last = pl.num_programs(0) - 1
    tm = cur_ref.shape[0]
    taps = w_ref.shape[0]
    halo = (taps // 2) * batch
    ext_ref[0:halo, :] = jnp.where(i > 0, prev_ref[tm - halo:tm, :], 0.0)
    ext_ref[halo:halo + tm, :] = cur_ref[...]
    ext_ref[halo + tm:halo + tm + halo, :] = jnp.where(i < last, next_ref[0:halo, :], 0.0)

    def body(c, _):
        r0 = pl.multiple_of(c * batch, batch)
        acc = jnp.zeros((batch, cur_ref.shape[1]), F32) + b_ref[...]
        for k in range(taps):
            acc = acc + w_ref[k:k + 1, :] * ext_ref[pl.ds(r0 + k * batch, batch), :]
        y = _layer_norm(acc, g_ref[...], beta_ref[...])
        o_ref[pl.ds(r0, batch), :] = y * jax.nn.sigmoid(y)
        return 0

    lax.fori_loop(0, tm // batch, body, 0)


def _dwconv_ln_silu(x, w, b, g, beta, batch):
    t, d = x.shape
    tm = TOKEN_TILE
    n_tiles = t // tm
    halo = (w.shape[0] // 2) * batch
    assert halo <= tm
    return pl.pallas_call(
        functools.partial(_dwconv_kernel, batch=batch),
        grid=(n_tiles,),
        in_specs=[
            _row_spec(tm, d),
            pl.BlockSpec((tm, d), lambda i: (jnp.maximum(i - 1, 0), 0)),
            pl.BlockSpec((tm, d), lambda i: (jnp.minimum(i + 1, n_tiles - 1), 0)),
            _full_spec(w.shape), _full_spec((1, d)), _full_spec((1, d)), _full_spec((1, d)),
        ],
        out_specs=_row_spec(tm, d),
        out_shape=jax.ShapeDtypeStruct((t, d), F32),
        scratch_shapes=[pltpu.VMEM((tm + 2 * halo, d), F32)],
        compiler_params=_params(("parallel",)),
        name="dwconv_ln_silu",
    )(x, x, x, w, b, g, beta)


def _tree(tiles, op):
    tiles = list(tiles)
    while len(tiles) > 1:
        tiles = [op(tiles[i], tiles[i + 1]) for i in range(0, len(tiles) - 1, 2)] + tiles[len(tiles) & ~1:]
    return tiles[0]


def _all_rows(tiles, op):
    x = _tree(tiles, op)
    for shift in (4, 2, 1):
        x = op(x, pltpu.roll(x, shift, axis=0))
    return x


def _rows_of(tiles8, sub):
    out = tiles8[SUBLANES - 1]
    for i in range(SUBLANES - 2, -1, -1):
        out = jnp.where(sub == i, tiles8[i], out)
    return out


def _top16_rows(s, row_id):
    vals, ids = [], []
    not_found = float(len(s) * SUBLANES)
    for _ in range(PEER_TOPK):
        m = _all_rows(s, jnp.maximum)
        am = _all_rows([jnp.where(x == m, i, not_found) for x, i in zip(s, row_id)], jnp.minimum)
        s = [jnp.where(i == am, -jnp.inf, x) for x, i in zip(s, row_id)]
        vals.append(m)
        ids.append(am)
    return vals, ids


def _candidates(r1, r2, sub):
    lo4 = sub < 4
    r2_lo, r2_hi, r1_hi = _rows_of(r2[:8], sub), _rows_of(r2[8:], sub), _rows_of(r1[8:], sub)
    r2_dup4 = jnp.where(lo4, r2_lo, pltpu.roll(r2_lo, 4, axis=0))
    left = [r1[0], r1[0], r1[1], r1[2], r1[3], jnp.where(lo4, r1[4], r1[5]), jnp.where(lo4, r1[6], r1[7]), r1_hi]
    right = [r2_lo, r2_hi, r2_lo, r2_lo, r2_lo, r2_dup4, r2_dup4, r2[0]]
    return left, right


def _cand_flat(sub):
    subf = sub.astype(F32)
    lo4 = sub < 4
    k = float(PEER_TOPK)
    return [subf, 8.0 + subf, k + subf, 2 * k + subf, 3 * k + subf,
            jnp.where(lo4, 4 * k + subf, 5 * k + subf - 4.0), jnp.where(lo4, 6 * k + subf, 7 * k + subf - 4.0),
            (8.0 + subf) * k]


def _route_kernel(q_ref, keys_ref, off_ref, gate_ref, cnt_ref, *, n_keys, half_experts_log2):
    half = keys_ref.shape[2]
    n_heads = q_ref.shape[1] // (2 * half)
    n_tiles = LANES // SUBLANES
    sub = lax.broadcasted_iota(jnp.int32, (SUBLANES, LANES), 0)
    subf = sub.astype(F32)
    key_id = [subf + float(v * SUBLANES) for v in range(n_keys // SUBLANES)]
    flat = _cand_flat(sub)
    contract_last = (((1,), (1,)), ((), ()))
    n_lo = jnp.zeros((SUBLANES, LANES), F32)
    n_hi = jnp.zeros((SUBLANES, LANES), F32)
    off_tiles = [jnp.zeros((SUBLANES, LANES), jnp.int32) for _ in range(n_tiles)]
    gate_tiles = [jnp.zeros((SUBLANES, LANES), F32) for _ in range(n_tiles)]
    for h in range(n_heads):
        tops = []
        for p in range(2):
            qp = q_ref[:, (2 * h + p) * half:(2 * h + p + 1) * half].astype(BF16)
            st = lax.dot_general(keys_ref[p], qp, contract_last, preferred_element_type=F32)
            tops.append(_top16_rows([st[v * SUBLANES:(v + 1) * SUBLANES, :] for v in range(n_keys // SUBLANES)], key_id))
        (v1, i1), (v2, i2) = tops
        cand = [a + b for a, b in zip(*_candidates(v1, v2, sub))]
        expert = [a * float(n_keys) + b for a, b in zip(*_candidates(i1, i2, sub))]
        s_rank, e_rank = [], []
        for _ in range(PEER_TOPK):
            m = _all_rows(cand, jnp.maximum)
            am = _all_rows([jnp.where(c == m, f, 256.0) for c, f in zip(cand, flat)], jnp.minimum)
            sel = [f == am for f in flat]
            e_rank.append(_all_rows([jnp.where(s, e, -1.0) for s, e in zip(sel, expert)], jnp.maximum))
            cand = [jnp.where(s, -jnp.inf, c) for s, c in zip(sel, cand)]
            s_rank.append(m)
        p = [jnp.exp(s - s_rank[0]) for s in s_rank]
        den = _tree(p, jnp.add)
        for r in range(PEER_TOPK):
            second_half = e_rank[r] >= float(1 << half_experts_log2)
            pos = jnp.where(second_half, float(LANES - 1) - n_hi, n_lo)
            n_hi = n_hi + jnp.where(second_half, 1.0, 0.0)
            n_lo = n_lo + jnp.where(second_half, 0.0, 1.0)
            e = e_rank[r].astype(jnp.int32)
            off = (e & ((1 << half_experts_log2) - 1)) * SUBLANES
            gate = p[r] / den
            gate = jnp.where(second_half, -gate, gate)
            for v in range(n_tiles):
                here = key_id[v] == pos
                off_tiles[v] = jnp.where(here, off, off_tiles[v])
                gate_tiles[v] = jnp.where(here, gate, gate_tiles[v])
    for v in range(n_tiles):
        rows = slice(v * SUBLANES, (v + 1) * SUBLANES)
        off_ref[rows, :] = off_tiles[v]
        gate_ref[rows, :] = gate_tiles[v]
    cnt_ref[...] = n_lo.astype(jnp.int32)


def _peer_route(q, keys, n_heads, half_experts_log2):
    t = q.shape[0]
    tb = LANES
    n_keys = keys.shape[1]
    assert n_heads * PEER_TOPK == LANES and n_keys == LANES and q.shape[1] == n_heads * 2 * keys.shape[2]
    out_spec = pl.BlockSpec((LANES, tb), lambda i: (0, i))
    return pl.pallas_call(
        functools.partial(_route_kernel, n_keys=n_keys, half_experts_log2=half_experts_log2),
        grid=(t // tb,),
        in_specs=[_row_spec(tb, q.shape[1]), _full_spec(keys.shape)],
        out_specs=[out_spec, out_spec, pl.BlockSpec((SUBLANES, tb), lambda i: (0, i))],
        out_shape=[jax.ShapeDtypeStruct((LANES, t), jnp.int32), jax.ShapeDtypeStruct((LANES, t), F32),
                   jax.ShapeDtypeStruct((SUBLANES, t), jnp.int32)],
        compiler_params=_params(("parallel",)),
        name="peer_route",
    )(q, keys)


_SLOT_OF_ROW = (0, 4, 2, 6, 1, 5, 3, 7)


def _sublane_sums(p, sub):
    lo4 = sub < 4
    c = []
    for a, b in ((p[0], p[1]), (p[2], p[3]), (p[4], p[5]), (p[6], p[7])):
        x = jnp.where(lo4, a, b)
        y = jnp.where(lo4, b, a)
        c.append(x + pltpu.roll(y, 4, axis=0))
    m2 = (sub & 2) == 0
    d = []
    for a, b in ((c[0], c[1]), (c[2], c[3])):
        d.append(jnp.where(m2, a + pltpu.roll(a, SUBLANES - 2, axis=0), b + pltpu.roll(b, 2, axis=0)))
    m1 = (sub & 1) == 0
    a, b = d
    return jnp.where(m1, a + pltpu.roll(a, SUBLANES - 1, axis=0), b + pltpu.roll(b, 1, axis=0))


def _lane_sums(r):
    hi = r.astype(BF16)
    lo = (r - hi.astype(F32)).astype(BF16)
    ones = jnp.ones((2 * LANES, LANES), BF16)
    return jnp.dot(jnp.concatenate([hi, lo], axis=1), ones, preferred_element_type=F32)


def _table_row(tbl_ref, off):
    return tbl_ref[pl.ds(pl.multiple_of(off, SUBLANES), SUBLANES), :]


N_POS_TILES = LANES // SUBLANES
WINDOW_TILES = 9
REST_GROUPS = ((9, 16),)
TOKEN_GROUP = 8
DOT_GROUP = 2


def _gather_specs(tb):
    offs = [pl.BlockSpec((SUBLANES, tb), lambda i, k=k: (k, i), memory_space=pltpu.SMEM) for k in range(N_POS_TILES)]
    return offs + [pl.BlockSpec((SUBLANES, tb), lambda i: (0, i), memory_space=pltpu.SMEM)]


def _tile_groups(second_half):
    order = list(range(N_POS_TILES))[::-1] if second_half else list(range(N_POS_TILES))
    return order[:WINDOW_TILES], [order[a:b] for a, b in REST_GROUPS]


def _group_needed(cnt_ref, t, second_half, group_index):
    n_first = cnt_ref[0, t]
    n_mine = LANES - n_first if second_half else n_first
    return n_mine > REST_GROUPS[group_index][0] * SUBLANES


def _peer_dot_kernel(*refs, second_half):
    off_ref = refs[:N_POS_TILES]
    cnt_ref, x_ref, tbl_ref, o_ref, ra_ref, rb_ref = refs[N_POS_TILES:]
    tb = x_ref.shape[0]
    sub = lax.broadcasted_iota(jnp.int32, (SUBLANES, LANES), 0)
    eye = (lax.broadcasted_iota(jnp.int32, (LANES, LANES), 0) == lax.broadcasted_iota(jnp.int32, (LANES, LANES), 1))
    window, groups = _tile_groups(second_half)

    def tile(t, x, k):
        prods = [_table_row(tbl_ref, off_ref[k][_SLOT_OF_ROW[j], t]) * x for j in range(SUBLANES)]
        return _sublane_sums(prods, sub)

    def rows(j, k):
        return slice(j * LANES + k * SUBLANES, j * LANES + (k + 1) * SUBLANES)

    def gather_windows(g0, r_ref):
        xs = [x_ref[g0 + j] for j in range(DOT_GROUP)]
        for k in window:
            for j, x in enumerate(xs):
                r_ref[rows(j, k), :] = tile(g0 + j, x, k)
        for j in range(DOT_GROUP):
            for tiles in groups:
                for k in tiles:
                    r_ref[rows(j, k), :] = jnp.zeros((SUBLANES, LANES), F32)

    def gather_rest(g0, r_ref):
        for j in range(DOT_GROUP):
            for gi, tiles in enumerate(groups):
                @pl.when(_group_needed(cnt_ref, g0 + j, second_half, gi))
                def _(j=j, tiles=tiles):
                    x = x_ref[g0 + j]
                    for k in tiles:
                        r_ref[rows(j, k), :] = tile(g0 + j, x, k)

    def finish(g0, r_ref):
        for j in range(DOT_GROUP):
            sums = jnp.sum(r_ref[j * LANES:(j + 1) * LANES, :], axis=1, keepdims=True)
            o_ref[pl.ds(g0 + j, 1), :] = jnp.sum(jnp.where(eye, sums, 0.0), axis=0, keepdims=True)

    rb_ref[...] = jnp.zeros_like(rb_ref)

    def group_pair(i, _):
        g0 = 2 * DOT_GROUP * i
        g1 = g0 + DOT_GROUP
        gather_windows(g0, ra_ref)
        finish(jnp.maximum(g0 - DOT_GROUP, 0), rb_ref)
        gather_rest(g0, ra_ref)
        gather_windows(g1, rb_ref)
        finish(g0, ra_ref)
        gather_rest(g1, rb_ref)
        return 0

    lax.fori_loop(0, tb // (2 * DOT_GROUP), group_pair, 0)
    finish(tb - DOT_GROUP, rb_ref)


def _peer_dots(offs_t, count, x3, tbl, second_half):
    tb = PEER_TOKEN_TILE
    t = x3.shape[0]
    return pl.pallas_call(
        functools.partial(_peer_dot_kernel, second_half=second_half),
        grid=(t // tb,),
        in_specs=_gather_specs(tb) + [pl.BlockSpec((tb, SUBLANES, LANES), lambda i: (i, 0, 0)),
                                      pl.BlockSpec(memory_space=pltpu.VMEM)],
        out_specs=pl.BlockSpec((tb, LANES), lambda i: (i, 0)),
        out_shape=jax.ShapeDtypeStruct((t, LANES), F32),
        scratch_shapes=[pltpu.VMEM((DOT_GROUP * LANES, LANES), F32)] * 2,
        compiler_params=_params(("parallel",)),
        name="peer_dots",
    )(*([offs_t] * N_POS_TILES), count, x3, tbl)


def _peer_weight_kernel(d0_ref, d1_ref, gate_ref, w0_ref, w1_ref):
    first = gate_ref[...] > 0.0
    w = jnp.abs(gate_ref[...]) * jax.nn.gelu(jnp.where(first, d0_ref[...], d1_ref[...]))
    w0_ref[...] = jnp.where(first, w, 0.0)
    w1_ref[...] = jnp.where(first, 0.0, w)


def _peer_weights(d0, d1, gates):
    t = d0.shape[0]
    tm = TOKEN_TILE
    out = jax.ShapeDtypeStruct((t, LANES), F32)
    return pl.pallas_call(
        _peer_weight_kernel,
        grid=(t // tm,),
        in_specs=[_row_spec(tm, LANES)] * 3,
        out_specs=[_row_spec(tm, LANES)] * 2,
        out_shape=[out, out],
        compiler_params=_params(("parallel",)),
        name="peer_weights",
    )(d0, d1, gates)


def _peer_mix_kernel(*refs, second_half):
    off_ref = refs[:N_POS_TILES]
    cnt_ref, w_ref, tbl_ref, o_ref, wa_ref, wb_ref = refs[N_POS_TILES:]
    tb = o_ref.shape[0]
    n_acc = 4
    eye = (lax.broadcasted_iota(jnp.int32, (LANES, LANES), 0) == lax.broadcasted_iota(jnp.int32, (LANES, LANES), 1))
    window, groups = _tile_groups(second_half)

    def spread(g0, dst_ref):
        w = w_ref[pl.ds(pl.multiple_of(g0, TOKEN_GROUP), TOKEN_GROUP), :]
        diag = [jnp.where(eye, jnp.broadcast_to(w[j:j + 1, :], (LANES, LANES)), 0.0) for j in range(TOKEN_GROUP)]
        dst_ref[...] = _lane_sums(jnp.concatenate(diag, axis=0))

    def weighted_rows(t, j, src_ref, tiles):
        acc = [jnp.zeros((SUBLANES, LANES), F32) for _ in range(n_acc)]
        for k in tiles:
            for s in range(SUBLANES):
                p = k * SUBLANES + s
                row = j * LANES + p
                wv = jnp.broadcast_to(src_ref[row:row + 1, :], (SUBLANES, LANES))
                acc[p % n_acc] = acc[p % n_acc] + wv * _table_row(tbl_ref, off_ref[k][s, t])
        return (acc[0] + acc[1]) + (acc[2] + acc[3])

    def gather(g0, src_ref):
        for j in range(TOKEN_GROUP):
            o_ref[g0 + j] = weighted_rows(g0 + j, j, src_ref, window)
        for j in range(TOKEN_GROUP):
            for gi, tiles in enumerate(groups):
                @pl.when(_group_needed(cnt_ref, g0 + j, second_half, gi))
                def _(j=j, tiles=tiles):
                    o_ref[g0 + j] = o_ref[g0 + j] + weighted_rows(g0 + j, j, src_ref, tiles)

    spread(0, wa_ref)

    def group_pair(i, _):
        g0 = 2 * TOKEN_GROUP * i
        g1 = g0 + TOKEN_GROUP
        spread(g1, wb_ref)
        gather(g0, wa_ref)
        spread(jnp.minimum(g1 + TOKEN_GROUP, tb - TOKEN_GROUP), wa_ref)
        gather(g1, wb_ref)
        return 0

    lax.fori_loop(0, tb // (2 * TOKEN_GROUP), group_pair, 0)


def _peer_mix(offs_t, count, w, tbl, second_half):
    tb = PEER_TOKEN_TILE
    t = w.shape[0]
    return pl.pallas_call(
        functools.partial(_peer_mix_kernel, second_half=second_half),
        grid=(t // tb,),
        in_specs=_gather_specs(tb) + [pl.BlockSpec((tb, LANES), lambda i: (i, 0)),
                                      pl.BlockSpec(memory_space=pltpu.VMEM)],
        out_specs=pl.BlockSpec((tb, SUBLANES, LANES), lambda i: (i, 0, 0)),
        out_shape=jax.ShapeDtypeStruct((t, SUBLANES, LANES), F32),
        scratch_shapes=[pltpu.VMEM((TOKEN_GROUP * LANES, LANES), F32)] * 2,
        compiler_params=_params(("parallel",)),
        name="peer_mix",
    )(*([offs_t] * N_POS_TILES), count, w, tbl)


def _add_ln_kernel(h_ref, f0_ref, f1_ref, g_ref, beta_ref, o_ref):
    y = DEEPNORM_ALPHA * h_ref[...] + (f0_ref[...] + f1_ref[...])
    o_ref[...] = _layer_norm(y, g_ref[...], beta_ref[...])


def _add_ln(h, f0, f1, g, beta):
    t, d = h.shape
    tm = TOKEN_TILE
    return pl.pallas_call(
        _add_ln_kernel,
        grid=(t // tm,),
        in_specs=[_row_spec(tm, d)] * 3 + [_full_spec((1, d))] * 2,
        out_specs=_row_spec(tm, d),
        out_shape=jax.ShapeDtypeStruct((t, d), F32),
        compiler_params=_params(("parallel",)),
        name="add_ln",
    )(h, f0, f1, g, beta)


def _peer_ffn(h, w_query, sub_keys, expert_u, expert_v, g, beta):
    t, d = h.shape
    n_experts = expert_u.shape[0]
    n_keys = sub_keys.shape[1]
    n_heads = w_query.shape[1] // (2 * sub_keys.shape[2])
    half = n_experts // 2
    assert d == SUBLANES * LANES and n_keys * n_keys == n_experts and half & (half - 1) == 0
    q = _matmul(h, w_query.astype(BF16))
    offs_t, gates_t, count = _peer_route(q, sub_keys.astype(BF16), n_heads, half.bit_length() - 1)
    gates = jnp.transpose(gates_t)
    x3 = h.reshape(t, SUBLANES, LANES)
    u2 = expert_u.reshape(n_experts * SUBLANES, LANES)
    v2 = expert_v.reshape(n_experts * SUBLANES, LANES)
    rows_half = half * SUBLANES
    d0 = _peer_dots(offs_t, count, x3, u2[:rows_half], False)
    d1 = _peer_dots(offs_t, count, x3, u2[rows_half:], True)
    w0, w1 = _peer_weights(d0, d1, gates)
    f0 = _peer_mix(offs_t, count, w0, v2[:rows_half], False).reshape(t, d)
    f1 = _peer_mix(offs_t, count, w1, v2[rows_half:], True).reshape(t, d)
    return _add_ln(h, f0, f1, g, beta)


def _row(v):
    return v.reshape(1, -1)


def kernel(x, meta_tokens, lru_w_in, lru_conv_w, lru_conv_b, lru_w_a, lru_b_a, lru_w_x, lru_b_x, lru_lambda, lru_w_out, conf_w_pw1, conf_b_pw1, conf_dw_w, conf_dw_b, conf_ln_g, conf_ln_b, conf_w_pw2, conf_b_pw2, peer_w_query, peer_sub_keys, peer_u, peer_v, ln_mix_g, ln_mix_b, ln_ffn_g, ln_ffn_b):
    bsz, seq, d = x.shape
    s_tot = N_META_TOKENS + seq
    t = s_tot * bsz
    assert bsz % SUBLANES == 0 and t % TOKEN_TILE == 0 and s_tot % SCAN_STEPS == 0
    assert TOKEN_TILE % (2 * bsz) == 0 and t % PEER_TOKEN_TILE == 0
    meta = jnp.broadcast_to(meta_tokens.astype(x.dtype)[None], (bsz, N_META_TOKENS, d))
    h = jnp.concatenate([meta, x], axis=1)
    h = jnp.transpose(h, (1, 0, 2)).reshape(t, d)

    for i in range(DEPTH):
        j = i // 2
        if i % 2 == 0:
            gu = _matmul(h, lru_w_in[j].astype(BF16))
            a_f, b_f, a_r, b_r = _lru_coefs(
                gu, lru_conv_w[j], _row(lru_conv_b[j]), lru_w_a[j].astype(BF16), lru_b_a[j],
                lru_w_x[j].astype(BF16), lru_b_x[j], lru_lambda[j], bsz)
            h_f, h_r = _lru_scan(a_f, b_f, a_r, b_r, bsz)
            h = _lru_out(h_f, h_r, gu, lru_w_out[j].astype(BF16), h, _row(ln_mix_g[i]), _row(ln_mix_b[i]))
        else:
            hg = _matmul_glu(h, conf_w_pw1[j].astype(BF16), _row(conf_b_pw1[j]))
            hc = _dwconv_ln_silu(hg, conf_dw_w[j], _row(conf_dw_b[j]), _row(conf_ln_g[j]), _row(conf_ln_b[j]), bsz)
            h = _matmul_res_ln(hc, conf_w_pw2[j].astype(BF16), _row(conf_b_pw2[j]), h,
                               _row(ln_mix_g[i]), _row(ln_mix_b[i]))
        h = _peer_ffn(h, peer_w_query[i], peer_sub_keys[i], peer_u[i], peer_v[i],
                      _row(ln_ffn_g[i]), _row(ln_ffn_b[i]))
    out = h.reshape(s_tot, bsz, d)[N_META_TOKENS:]
    return jnp.transpose(out, (1, 0, 2))
```

```python
import functools

import jax
import jax.numpy as jnp
import numpy as np
from jax import lax
from jax.experimental import pallas as pl
from jax.experimental.pallas import tpu as pltpu

F32 = jnp.float32
BF16 = jnp.bfloat16
HIGHEST = lax.Precision.HIGHEST

N_META_TOKENS = 16
LRU_C = 8.0
LN_EPS = 1e-5
DEPTH = 2
DEEPNORM_ALPHA = (2.0 * DEPTH) ** 0.25

SUBLANES = 8
LANES = 128
VMEM_LIMIT_BYTES = 56 * 1024 * 1024

TOKEN_TILE = 512
SCAN_STEPS = 48
SCAN_CH = 256
PEER_TOKEN_TILE = 128
PEER_TOPK = 16


def _params(semantics):
    return pltpu.CompilerParams(dimension_semantics=semantics, vmem_limit_bytes=VMEM_LIMIT_BYTES)


def _mxu(a, b):
    return jnp.dot(a.astype(BF16), b.astype(BF16), preferred_element_type=F32)


def _layer_norm(y, g, b):
    mu = jnp.mean(y, axis=-1, keepdims=True)
    d = y - mu
    var = jnp.mean(d * d, axis=-1, keepdims=True)
    return d * lax.rsqrt(var + LN_EPS) * g + b


def _expm1(x):
    series = x * (1.0 + x * (1 / 2 + x * (1 / 6 + x * (1 / 24 + x * (1 / 120 + x * (1 / 720))))))
    return jnp.where(jnp.abs(x) < 0.1, series, jnp.exp(x) - 1.0)


def _row_spec(tm, n, col=0):
    return pl.BlockSpec((tm, n), lambda i: (i, col))


def _full_spec(shape):
    nd = len(shape)
    return pl.BlockSpec(shape, lambda i: (0,) * nd)


def _mm_kernel(x_ref, w_ref, o_ref):
    o_ref[...] = _mxu(x_ref[...], w_ref[...])


def _matmul(x, w):
    t, k = x.shape
    n = w.shape[1]
    return pl.pallas_call(
        _mm_kernel,
        grid=(t // TOKEN_TILE,),
        in_specs=[_row_spec(TOKEN_TILE, k), _full_spec((k, n))],
        out_specs=_row_spec(TOKEN_TILE, n),
        out_shape=jax.ShapeDtypeStruct((t, n), F32),
        compiler_params=_params(("parallel",)),
        name="matmul",
    )(x, w)


def _mm_glu_kernel(x_ref, w_ref, b_ref, o_ref):
    z = _mxu(x_ref[...], w_ref[...]) + b_ref[...]
    d = o_ref.shape[1]
    o_ref[...] = z[:, :d] * jax.nn.sigmoid(z[:, d:])


def _matmul_glu(x, w, b):
    t, k = x.shape
    n = w.shape[1]
    return pl.pallas_call(
        _mm_glu_kernel,
        grid=(t // TOKEN_TILE,),
        in_specs=[_row_spec(TOKEN_TILE, k), _full_spec((k, n)), _full_spec((1, n))],
        out_specs=_row_spec(TOKEN_TILE, n // 2),
        out_shape=jax.ShapeDtypeStruct((t, n // 2), F32),
        compiler_params=_params(("parallel",)),
        name="matmul_glu",
    )(x, w, b)


def _mm_res_ln_kernel(a_ref, w_ref, b_ref, res_ref, g_ref, beta_ref, o_ref):
    y = DEEPNORM_ALPHA * res_ref[...] + _mxu(a_ref[...], w_ref[...]) + b_ref[...]
    o_ref[...] = _layer_norm(y, g_ref[...], beta_ref[...])


def _matmul_res_ln(a, w, b, res, g, beta):
    t, k = a.shape
    n = w.shape[1]
    return pl.pallas_call(
        _mm_res_ln_kernel,
        grid=(t // TOKEN_TILE,),
        in_specs=[_row_spec(TOKEN_TILE, k), _full_spec((k, n)), _full_spec((1, n)),
                  _row_spec(TOKEN_TILE, n), _full_spec((1, n)), _full_spec((1, n))],
        out_specs=_row_spec(TOKEN_TILE, n),
        out_shape=jax.ShapeDtypeStruct((t, n), F32),
        compiler_params=_params(("parallel",)),
        name="matmul_res_ln",
    )(a, w, b, res, g, beta)


def _lru_coef_kernel(u_ref, prev_ref, next_ref, cw_ref, cb_ref, wa_ref, ba_ref, wx_ref, bx_ref,
                     lam_ref, af_ref, bf_ref, ar_ref, br_ref, *, batch):
    i = pl.program_id(0)
    last = pl.num_programs(0) - 1
    tm = u_ref.shape[0]
    prev = jnp.where(i > 0, prev_ref[...], 0.0)
    nxt = jnp.where(i < last, next_ref[...], 0.0)
    ext = jnp.concatenate([prev, u_ref[...], nxt], axis=0)
    uc = cb_ref[...] + sum(cw_ref[k:k + 1, :] * ext[k * batch:k * batch + tm, :] for k in range(4))
    n_blocks = wa_ref.shape[1]
    bw = wa_ref.shape[2]
    outs = ((af_ref, bf_ref), (ar_ref, br_ref))
    for n in range(n_blocks):
        sl = slice(n * bw, (n + 1) * bw)
        ub = uc[:, sl]
        ub16 = ub.astype(BF16)
        for d in range(2):
            r = jax.nn.sigmoid(jnp.dot(ub16, wa_ref[d, n], preferred_element_type=F32) + ba_ref[d:d + 1, sl])
            g = jax.nn.sigmoid(jnp.dot(ub16, wx_ref[d, n], preferred_element_type=F32) + bx_ref[d:d + 1, sl])
            log_a = (-LRU_C * r) * jax.nn.softplus(-lam_ref[d:d + 1, sl])
            a_ref, b_ref = outs[d]
            a_ref[:, sl] = jnp.exp(log_a)
            b_ref[:, sl] = jnp.sqrt(-_expm1(2.0 * log_a)) * (g * ub)


def _lru_coefs(gu, conv_w, conv_b, w_a, b_a, w_x, b_x, lam, batch):
    t = gu.shape[0]
    d = conv_w.shape[1]
    tm = TOKEN_TILE
    n_prev = tm // (2 * batch)
    n_next = tm // batch
    last_next = t // batch - 1
    out = jax.ShapeDtypeStruct((t, d), F32)
    return pl.pallas_call(
        functools.partial(_lru_coef_kernel, batch=batch),
        grid=(t // tm,),
        in_specs=[
            pl.BlockSpec((tm, d), lambda i: (i, 1)),
            pl.BlockSpec((2 * batch, d), lambda i: (jnp.maximum(i * n_prev - 1, 0), 1)),
            pl.BlockSpec((batch, d), lambda i: (jnp.minimum((i + 1) * n_next, last_next), 1)),
            _full_spec(conv_w.shape), _full_spec((1, d)),
            _full_spec(w_a.shape), _full_spec(b_a.shape),
            _full_spec(w_x.shape), _full_spec(b_x.shape), _full_spec(lam.shape),
        ],
        out_specs=[_row_spec(tm, d)] * 4,
        out_shape=[out] * 4,
        compiler_params=_params(("parallel",)),
        name="lru_coefs",
    )(gu, gu, gu, conv_w, conv_b, w_a, b_a, w_x, b_x, lam)


def _lru_scan_kernel(af_ref, bf_ref, ar_ref, br_ref, hf_ref, hr_ref, cf_ref, cr_ref, *, batch):
    j = pl.program_id(1)

    @pl.when(j == 0)
    def _():
        cf_ref[...] = jnp.zeros_like(cf_ref)
        cr_ref[...] = jnp.zeros_like(cr_ref)

    steps = af_ref.shape[0] // batch

    def body(s, carry):
        hf, hr = carry
        rf = pl.multiple_of(s * batch, batch)
        hf = af_ref[pl.ds(rf, batch), :] * hf + bf_ref[pl.ds(rf, batch), :]
        hf_ref[pl.ds(rf, batch), :] = hf
        rr = pl.multiple_of((steps - 1 - s) * batch, batch)
        hr = ar_ref[pl.ds(rr, batch), :] * hr + br_ref[pl.ds(rr, batch), :]
        hr_ref[pl.ds(rr, batch), :] = hr
        return hf, hr

    hf, hr = lax.fori_loop(0, steps, body, (cf_ref[...], cr_ref[...]), unroll=4)
    cf_ref[...] = hf
    cr_ref[...] = hr


def _lru_scan(a_f, b_f, a_r, b_r, batch):
    t, d = a_f.shape
    rows = SCAN_STEPS * batch
    n_chunks = t // rows
    fwd = pl.BlockSpec((rows, SCAN_CH), lambda c, j: (j, c))
    rev = pl.BlockSpec((rows, SCAN_CH), lambda c, j: (n_chunks - 1 - j, c))
    out = jax.ShapeDtypeStruct((t, d), F32)
    return pl.pallas_call(
        functools.partial(_lru_scan_kernel, batch=batch),
        grid=(d // SCAN_CH, n_chunks),
        in_specs=[fwd, fwd, rev, rev],
        out_specs=[fwd, rev],
        out_shape=[out, out],
        scratch_shapes=[pltpu.VMEM((batch, SCAN_CH), F32), pltpu.VMEM((batch, SCAN_CH), F32)],
        compiler_params=_params(("parallel", "arbitrary")),
        name="lru_scan",
    )(a_f, b_f, a_r, b_r)


def _lru_out_kernel(hf_ref, hr_ref, gate_ref, w_ref, res_ref, g_ref, beta_ref, o_ref):
    a = (hf_ref[...] + hr_ref[...]) * jax.nn.gelu(gate_ref[...])
    y = DEEPNORM_ALPHA * res_ref[...] + _mxu(a, w_ref[...])
    o_ref[...] = _layer_norm(y, g_ref[...], beta_ref[...])


def _lru_out(h_f, h_r, gu, w_out, res, g, beta):
    t, d = h_f.shape
    n = w_out.shape[1]
    tm = TOKEN_TILE
    return pl.pallas_call(
        _lru_out_kernel,
        grid=(t // tm,),
        in_specs=[_row_spec(tm, d), _row_spec(tm, d), _row_spec(tm, d, col=0), _full_spec((d, n)),
                  _row_spec(tm, n), _full_spec((1, n)), _full_spec((1, n))],
        out_specs=_row_spec(tm, n),
        out_shape=jax.ShapeDtypeStruct((t, n), F32),
        compiler_params=_params(("parallel",)),
        name="lru_out",
    )(h_f, h_r, gu, w_out, res, g, beta)


def _dwconv_kernel(cur_ref, prev_ref, next_ref, w_ref, b_ref, g_ref, beta_ref, o_ref, ext_ref, *, batch):
    i = pl.program_id(0)
    last = pl.num_programs(0) - 1
    tm = cur_ref.shape[0]
    taps = w_ref.shape[0]
    halo = (taps // 2) * batch
    ext_ref[0:halo, :] = jnp.where(i > 0, prev_ref[tm - halo:tm, :], 0.0)
    ext_ref[halo:halo + tm, :] = cur_ref[...]
    ext_ref[halo + tm:halo + tm + halo, :] = jnp.where(i < last, next_ref[0:halo, :], 0.0)

    def body(c, _):
        r0 = pl.multiple_of(c * batch, batch)
        acc = jnp.zeros((batch, cur_ref.shape[1]), F32) + b_ref[...]
        for k in range(taps):
            acc = acc + w_ref[k:k + 1, :] * ext_ref[pl.ds(r0 + k * batch, batch), :]
        y = _layer_norm(acc, g_ref[...], beta_ref[...])
        o_ref[pl.ds(r0, batch), :] = y * jax.nn.sigmoid(y)
        return 0

    lax.fori_loop(0, tm // batch, body, 0)


def _dwconv_ln_silu(x, w, b, g, beta, batch):
    t, d = x.shape
    tm = TOKEN_TILE
    n_tiles = t // tm
    halo = (w.shape[0] // 2) * batch
    assert halo <= tm
    return pl.pallas_call(
        functools.partial(_dwconv_kernel, batch=batch),
        grid=(n_tiles,),
        in_specs=[
            _row_spec(tm, d),
            pl.BlockSpec((tm, d), lambda i: (jnp.maximum(i - 1, 0), 0)),
            pl.BlockSpec((tm, d), lambda i: (jnp.minimum(i + 1, n_tiles - 1), 0)),
            _full_spec(w.shape), _full_spec((1, d)), _full_spec((1, d)), _full_spec((1, d)),
        ],
        out_specs=_row_spec(tm, d),
        out_shape=jax.ShapeDtypeStruct((t, d), F32),
        scratch_shapes=[pltpu.VMEM((tm + 2 * halo, d), F32)],
        compiler_params=_params(("parallel",)),
        name="dwconv_ln_silu",
    )(x, x, x, w, b, g, beta)


def _tree(tiles, op):
    tiles = list(tiles)
    while len(tiles) > 1:
        tiles = [op(tiles[i], tiles[i + 1]) for i in range(0, len(tiles) - 1, 2)] + tiles[len(tiles) & ~1:]
    return tiles[0]


def _all_rows(tiles, op):
    x = _tree(tiles, op)
    for shift in (4, 2, 1):
        x = op(x, pltpu.roll(x, shift, axis=0))
    return x


def _rows_of(tiles8, sub):
    out = tiles8[SUBLANES - 1]
    for i in range(SUBLANES - 2, -1, -1):
        out = jnp.where(sub == i, tiles8[i], out)
    return out


def _top16_rows(s, row_id):
    vals, ids = [], []
    not_found = float(len(s) * SUBLANES)
    for _ in range(PEER_TOPK):
        m = _all_rows(s, jnp.maximum)
        am = _all_rows([jnp.where(x == m, i, not_found) for x, i in zip(s, row_id)], jnp.minimum)
        s = [jnp.where(i == am, -jnp.inf, x) for x, i in zip(s, row_id)]
        vals.append(m)
        ids.append(am)
    return vals, ids


def _candidates(r1, r2, sub):
    lo4 = sub < 4
    r2_lo, r2_hi, r1_hi = _rows_of(r2[:8], sub), _rows_of(r2[8:], sub), _rows_of(r1[8:], sub)
    r2_dup4 = jnp.where(lo4, r2_lo, pltpu.roll(r2_lo, 4, axis=0))
    left = [r1[0], r1[0], r1[1], r1[2], r1[3], jnp.where(lo4, r1[4], r1[5]), jnp.where(lo4, r1[6], r1[7]), r1_hi]
    right = [r2_lo, r2_hi, r2_lo, r2_lo, r2_lo, r2_dup4, r2_dup4, r2[0]]
    return left, right


def _cand_flat(sub):
    subf = sub.astype(F32)
    lo4 = sub < 4
    k = float(PEER_TOPK)
    return [subf, 8.0 + subf, k + subf, 2 * k + subf, 3 * k + subf,
            jnp.where(lo4, 4 * k + subf, 5 * k + subf - 4.0), jnp.where(lo4, 6 * k + subf, 7 * k + subf - 4.0),
            (8.0 + subf) * k]


def _route_kernel(q_ref, keys_ref, off_ref, gate_ref, cnt_ref, *, n_keys):
    half = keys_ref.shape[2]
    keys_log2 = n_keys.bit_length() - 1
    n_heads = q_ref.shape[1] // (2 * half)
    n_tiles = LANES // SUBLANES
    sub = lax.broadcasted_iota(jnp.int32, (SUBLANES, LANES), 0)
    subf = sub.astype(F32)
    key_id = [subf + float(v * SUBLANES) for v in range(n_keys // SUBLANES)]
    flat = _cand_flat(sub)
    contract_last = (((1,), (1,)), ((), ()))
    n_lo = jnp.zeros((SUBLANES, LANES), F32)
    n_hi = jnp.zeros((SUBLANES, LANES), F32)
    off_tiles = [jnp.zeros((SUBLANES, LANES), jnp.int32) for _ in range(n_tiles)]
    gate_tiles = [jnp.zeros((SUBLANES, LANES), F32) for _ in range(n_tiles)]
    for h in range(n_heads):
        tops = []
        for p in range(2):
            qp = q_ref[:, (2 * h + p) * half:(2 * h + p + 1) * half].astype(BF16)
            st = lax.dot_general(keys_ref[p], qp, contract_last, preferred_element_type=F32)
            tops.append(_top16_rows([st[v * SUBLANES:(v + 1) * SUBLANES, :] for v in range(n_keys // SUBLANES)], key_id))
        (v1, i1), (v2, i2) = tops
        cand = [a + b for a, b in zip(*_candidates(v1, v2, sub))]
        expert = [a * float(n_keys) + b for a, b in zip(*_candidates(i1, i2, sub))]
        s_rank, e_rank = [], []
        for _ in range(PEER_TOPK):
            m = _all_rows(cand, jnp.maximum)
            am = _all_rows([jnp.where(c == m, f, 256.0) for c, f in zip(cand, flat)], jnp.minimum)
            sel = [f == am for f in flat]
            e_rank.append(_all_rows([jnp.where(s, e, -1.0) for s, e in zip(sel, expert)], jnp.maximum))
            cand = [jnp.where(s, -jnp.inf, c) for s, c in zip(sel, cand)]
            s_rank.append(m)
        p = [jnp.exp(s - s_rank[0]) for s in s_rank]
        den = _tree(p, jnp.add)
        for r in range(PEER_TOPK):
            e = e_rank[r].astype(jnp.int32)
            i1 = e >> keys_log2
            i2 = e & (n_keys - 1)
            second_half = ((i1 ^ i2) & 1) == 1
            pos = jnp.where(second_half, float(LANES - 1) - n_hi, n_lo)
            n_hi = n_hi + jnp.where(second_half, 1.0, 0.0)
            n_lo = n_lo + jnp.where(second_half, 0.0, 1.0)
            off = (i1 * (n_keys // 2) + (i2 >> 1)) * SUBLANES
            gate = p[r] / den
            gate = jnp.where(second_half, -gate, gate)
            for v in range(n_tiles):
                here = key_id[v] == pos
                off_tiles[v] = jnp.where(here, off, off_tiles[v])
                gate_tiles[v] = jnp.where(here, gate, gate_tiles[v])
    for v in range(n_tiles):
        rows = slice(v * SUBLANES, (v + 1) * SUBLANES)
        off_ref[rows, :] = off_tiles[v]
        gate_ref[rows, :] = gate_tiles[v]
    cnt_ref[...] = n_lo.astype(jnp.int32)


def _peer_route(q, keys, n_heads):
    t = q.shape[0]
    tb = LANES
    n_keys = keys.shape[1]
    assert n_heads * PEER_TOPK == LANES and n_keys == LANES and q.shape[1] == n_heads * 2 * keys.shape[2]
    out_spec = pl.BlockSpec((LANES, tb), lambda i: (0, i))
    return pl.pallas_call(
        functools.partial(_route_kernel, n_keys=n_keys),
        grid=(t // tb,),
        in_specs=[_row_spec(tb, q.shape[1]), _full_spec(keys.shape)],
        out_specs=[out_spec, out_spec, pl.BlockSpec((SUBLANES, tb), lambda i: (0, i))],
        out_shape=[jax.ShapeDtypeStruct((LANES, t), jnp.int32), jax.ShapeDtypeStruct((LANES, t), F32),
                   jax.ShapeDtypeStruct((SUBLANES, t), jnp.int32)],
        compiler_params=_params(("parallel",)),
        name="peer_route",
    )(q, keys)


_SLOT_OF_ROW = (0, 4, 2, 6, 1, 5, 3, 7)


def _sublane_sums(p, sub):
    lo4 = sub < 4
    c = []
    for a, b in ((p[0], p[1]), (p[2], p[3]), (p[4], p[5]), (p[6], p[7])):
        x = jnp.where(lo4, a, b)
        y = jnp.where(lo4, b, a)
        c.append(x + pltpu.roll(y, 4, axis=0))
    m2 = (sub & 2) == 0
    d = []
    for a, b in ((c[0], c[1]), (c[2], c[3])):
        d.append(jnp.where(m2, a + pltpu.roll(a, SUBLANES - 2, axis=0), b + pltpu.roll(b, 2, axis=0)))
    m1 = (sub & 1) == 0
    a, b = d
    return jnp.where(m1, a + pltpu.roll(a, SUBLANES - 1, axis=0), b + pltpu.roll(b, 1, axis=0))


def _lane_sums(r):
    hi = r.astype(BF16)
    lo = (r - hi.astype(F32)).astype(BF16)
    ones = jnp.ones((2 * LANES, LANES), BF16)
    return jnp.dot(jnp.concatenate([hi, lo], axis=1), ones, preferred_element_type=F32)


def _table_row(tbl_ref, off):
    return tbl_ref[pl.ds(pl.multiple_of(off, SUBLANES), SUBLANES), :]


N_POS_TILES = LANES // SUBLANES
WINDOW_TILES = 9
REST_GROUPS = ((9, 16),)
TOKEN_GROUP = 8
DOT_GROUP = 2


def _gather_specs(tb):
    offs = [pl.BlockSpec((SUBLANES, tb), lambda i, k=k: (k, i), memory_space=pltpu.SMEM) for k in range(N_POS_TILES)]
    return offs + [pl.BlockSpec((SUBLANES, tb), lambda i: (0, i), memory_space=pltpu.SMEM)]


def _tile_groups(second_half):
    order = list(range(N_POS_TILES))[::-1] if second_half else list(range(N_POS_TILES))
    return order[:WINDOW_TILES], [order[a:b] for a, b in REST_GROUPS]


def _group_needed(cnt_ref, t, second_half, group_index):
    n_first = cnt_ref[0, t]
    n_mine = LANES - n_first if second_half else n_first
    return n_mine > REST_GROUPS[group_index][0] * SUBLANES


def _peer_dot_kernel(*refs, second_half):
    off_ref = refs[:N_POS_TILES]
    cnt_ref, x_ref, tbl_ref, o_ref, ra_ref, rb_ref = refs[N_POS_TILES:]
    tb = x_ref.shape[0]
    sub = lax.broadcasted_iota(jnp.int32, (SUBLANES, LANES), 0)
    eye = (lax.broadcasted_iota(jnp.int32, (LANES, LANES), 0) == lax.broadcasted_iota(jnp.int32, (LANES, LANES), 1))
    window, groups = _tile_groups(second_half)

    def tile(t, x, k):
        prods = [_table_row(tbl_ref, off_ref[k][_SLOT_OF_ROW[j], t]) * x for j in range(SUBLANES)]
        return _sublane_sums(prods, sub)

    def rows(j, k):
        return slice(j * LANES + k * SUBLANES, j * LANES + (k + 1) * SUBLANES)

    def gather_windows(g0, r_ref):
        xs = [x_ref[g0 + j] for j in range(DOT_GROUP)]
        for k in window:
            for j, x in enumerate(xs):
                r_ref[rows(j, k), :] = tile(g0 + j, x, k)
        for j in range(DOT_GROUP):
            for tiles in groups:
                for k in tiles:
                    r_ref[rows(j, k), :] = jnp.zeros((SUBLANES, LANES), F32)

    def gather_rest(g0, r_ref):
        for j in range(DOT_GROUP):
            for gi, tiles in enumerate(groups):
                @pl.when(_group_needed(cnt_ref, g0 + j, second_half, gi))
                def _(j=j, tiles=tiles):
                    x = x_ref[g0 + j]
                    for k in tiles:
                        r_ref[rows(j, k), :] = tile(g0 + j, x, k)

    def finish(g0, r_ref):
        for j in range(DOT_GROUP):
            sums = jnp.sum(r_ref[j * LANES:(j + 1) * LANES, :], axis=1, keepdims=True)
            o_ref[pl.ds(g0 + j, 1), :] = jnp.sum(jnp.where(eye, sums, 0.0), axis=0, keepdims=True)

    rb_ref[...] = jnp.zeros_like(rb_ref)

    def group_pair(i, _):
        g0 = 2 * DOT_GROUP * i
        g1 = g0 + DOT_GROUP
        gather_windows(g0, ra_ref)
        finish(jnp.maximum(g0 - DOT_GROUP, 0), rb_ref)
        gather_rest(g0, ra_ref)
        gather_windows(g1, rb_ref)
        finish(g0, ra_ref)
        gather_rest(g1, rb_ref)
        return 0

    lax.fori_loop(0, tb // (2 * DOT_GROUP), group_pair, 0)
    finish(tb - DOT_GROUP, rb_ref)


def _peer_dots(offs_t, count, x3, tbl, second_half):
    tb = PEER_TOKEN_TILE
    t = x3.shape[0]
    return pl.pallas_call(
        functools.partial(_peer_dot_kernel, second_half=second_half),
        grid=(t // tb,),
        in_specs=_gather_specs(tb) + [pl.BlockSpec((tb, SUBLANES, LANES), lambda i: (i, 0, 0)),
                                      pl.BlockSpec(memory_space=pltpu.VMEM)],
        out_specs=pl.BlockSpec((tb, LANES), lambda i: (i, 0)),
        out_shape=jax.ShapeDtypeStruct((t, LANES), F32),
        scratch_shapes=[pltpu.VMEM((DOT_GROUP * LANES, LANES), F32)] * 2,
        compiler_params=_params(("parallel",)),
        name="peer_dots",
    )(*([offs_t] * N_POS_TILES), count, x3, tbl)


def _peer_weight_kernel(d0_ref, d1_ref, gate_ref, w0_ref, w1_ref):
    first = gate_ref[...] > 0.0
    w = jnp.abs(gate_ref[...]) * jax.nn.gelu(jnp.where(first, d0_ref[...], d1_ref[...]))
    w0_ref[...] = jnp.where(first, w, 0.0)
    w1_ref[...] = jnp.where(first, 0.0, w)


def _peer_weights(d0, d1, gates):
    t = d0.shape[0]
    tm = TOKEN_TILE
    out = jax.ShapeDtypeStruct((t, LANES), F32)
    return pl.pallas_call(
        _peer_weight_kernel,
        grid=(t // tm,),
        in_specs=[_row_spec(tm, LANES)] * 3,
        out_specs=[_row_spec(tm, LANES)] * 2,
        out_shape=[out, out],
        compiler_params=_params(("parallel",)),
        name="peer_weights",
    )(d0, d1, gates)


def _peer_mix_kernel(*refs, second_half):
    off_ref = refs[:N_POS_TILES]
    cnt_ref, w_ref, tbl_ref, o_ref, wa_ref, wb_ref = refs[N_POS_TILES:]
    tb = o_ref.shape[0]
    n_acc = 4
    eye = (lax.broadcasted_iota(jnp.int32, (LANES, LANES), 0) == lax.broadcasted_iota(jnp.int32, (LANES, LANES), 1))
    window, groups = _tile_groups(second_half)

    def spread(g0, dst_ref):
        w = w_ref[pl.ds(pl.multiple_of(g0, TOKEN_GROUP), TOKEN_GROUP), :]
        diag = [jnp.where(eye, jnp.broadcast_to(w[j:j + 1, :], (LANES, LANES)), 0.0) for j in range(TOKEN_GROUP)]
        dst_ref[...] = _lane_sums(jnp.concatenate(diag, axis=0))

    def weighted_rows(t, j, src_ref, tiles):
        acc = [jnp.zeros((SUBLANES, LANES), F32) for _ in range(n_acc)]
        for k in tiles:
            for s in range(SUBLANES):
                p = k * SUBLANES + s
                row = j * LANES + p
                wv = jnp.broadcast_to(src_ref[row:row + 1, :], (SUBLANES, LANES))
                acc[p % n_acc] = acc[p % n_acc] + wv * _table_row(tbl_ref, off_ref[k][s, t])
        return (acc[0] + acc[1]) + (acc[2] + acc[3])

    def gather(g0, src_ref):
        for j in range(TOKEN_GROUP):
            o_ref[g0 + j] = weighted_rows(g0 + j, j, src_ref, window)
        for j in range(TOKEN_GROUP):
            for gi, tiles in enumerate(groups):
                @pl.when(_group_needed(cnt_ref, g0 + j, second_half, gi))
                def _(j=j, tiles=tiles):
                    o_ref[g0 + j] = o_ref[g0 + j] + weighted_rows(g0 + j, j, src_ref, tiles)

    spread(0, wa_ref)

    def group_pair(i, _):
        g0 = 2 * TOKEN_GROUP * i
        g1 = g0 + TOKEN_GROUP
        spread(g1, wb_ref)
        gather(g0, wa_ref)
        spread(jnp.minimum(g1 + TOKEN_GROUP, tb - TOKEN_GROUP), wa_ref)
        gather(g1, wb_ref)
        return 0

    lax.fori_loop(0, tb // (2 * TOKEN_GROUP), group_pair, 0)


def _peer_mix(offs_t, count, w, tbl, second_half):
    tb = PEER_TOKEN_TILE
    t = w.shape[0]
    return pl.pallas_call(
        functools.partial(_peer_mix_kernel, second_half=second_half),
        grid=(t // tb,),
        in_specs=_gather_specs(tb) + [pl.BlockSpec((tb, LANES), lambda i: (i, 0)),
                                      pl.BlockSpec(memory_space=pltpu.VMEM)],
        out_specs=pl.BlockSpec((tb, SUBLANES, LANES), lambda i: (i, 0, 0)),
        out_shape=jax.ShapeDtypeStruct((t, SUBLANES, LANES), F32),
        scratch_shapes=[pltpu.VMEM((TOKEN_GROUP * LANES, LANES), F32)] * 2,
        compiler_params=_params(("parallel",)),
        name="peer_mix",
    )(*([offs_t] * N_POS_TILES), count, w, tbl)


def _add_ln_kernel(h_ref, f0_ref, f1_ref, g_ref, beta_ref, o_ref):
    y = DEEPNORM_ALPHA * h_ref[...] + (f0_ref[...] + f1_ref[...])
    o_ref[...] = _layer_norm(y, g_ref[...], beta_ref[...])


def _add_ln(h, f0, f1, g, beta):
    t, d = h.shape
    tm = TOKEN_TILE
    return pl.pallas_call(
        _add_ln_kernel,
        grid=(t // tm,),
        in_specs=[_row_spec(tm, d)] * 3 + [_full_spec((1, d))] * 2,
        out_specs=_row_spec(tm, d),
        out_shape=jax.ShapeDtypeStruct((t, d), F32),
        compiler_params=_params(("parallel",)),
        name="add_ln",
    )(h, f0, f1, g, beta)


def _split_by_parity(table, n_keys):
    d = table.shape[1]
    t5 = table.reshape(n_keys // 2, 2, n_keys // 2, 2, d)
    even = jnp.stack([t5[:, 0, :, 0], t5[:, 1, :, 1]], axis=1)
    odd = jnp.stack([t5[:, 0, :, 1], t5[:, 1, :, 0]], axis=1)
    rows = (n_keys * n_keys // 2) * (d // LANES)
    return even.reshape(rows, LANES), odd.reshape(rows, LANES)


def _peer_ffn(h, w_query, sub_keys, expert_u, expert_v, g, beta):
    t, d = h.shape
    n_experts = expert_u.shape[0]
    n_keys = sub_keys.shape[1]
    n_heads = w_query.shape[1] // (2 * sub_keys.shape[2])
    assert d == SUBLANES * LANES and n_keys * n_keys == n_experts and n_keys & (n_keys - 1) == 0
    q = _matmul(h, w_query.astype(BF16))
    offs_t, gates_t, count = _peer_route(q, sub_keys.astype(BF16), n_heads)
    gates = jnp.transpose(gates_t)
    x3 = h.reshape(t, SUBLANES, LANES)
    u_even, u_odd = _split_by_parity(expert_u, n_keys)
    v_even, v_odd = _split_by_parity(expert_v, n_keys)
    d0 = _peer_dots(offs_t, count, x3, u_even, False)
    d1 = _peer_dots(offs_t, count, x3, u_odd, True)
    w0, w1 = _peer_weights(d0, d1, gates)
    f0 = _peer_mix(offs_t, count, w0, v_even, False).reshape(t, d)
    f1 = _peer_mix(offs_t, count, w1, v_odd, True).reshape(t, d)
    return _add_ln(h, f0, f1, g, beta)


def _row(v):
    return v.reshape(1, -1)


def kernel(x, meta_tokens, lru_w_in, lru_conv_w, lru_conv_b, lru_w_a, lru_b_a, lru_w_x, lru_b_x, lru_lambda, lru_w_out, conf_w_pw1, conf_b_pw1, conf_dw_w, conf_dw_b, conf_ln_g, conf_ln_b, conf_w_pw2, conf_b_pw2, peer_w_query, peer_sub_keys, peer_u, peer_v, ln_mix_g, ln_mix_b, ln_ffn_g, ln_ffn_b):
    bsz, seq, d = x.shape
    s_tot = N_META_TOKENS + seq
    t = s_tot * bsz
    assert bsz % SUBLANES == 0 and t % TOKEN_TILE == 0 and s_tot % SCAN_STEPS == 0
    assert TOKEN_TILE % (2 * bsz) == 0 and t % PEER_TOKEN_TILE == 0
    meta = jnp.broadcast_to(meta_tokens.astype(x.dtype)[None], (bsz, N_META_TOKENS, d))
    h = jnp.concatenate([meta, x], axis=1)
    h = jnp.transpose(h, (1, 0, 2)).reshape(t, d)

    for i in range(DEPTH):
        j = i // 2
        if i % 2 == 0:
            gu = _matmul(h, lru_w_in[j].astype(BF16))
            a_f, b_f, a_r, b_r = _lru_coefs(
                gu, lru_conv_w[j], _row(lru_conv_b[j]), lru_w_a[j].astype(BF16), lru_b_a[j],
                lru_w_x[j].astype(BF16), lru_b_x[j], lru_lambda[j], bsz)
            h_f, h_r = _lru_scan(a_f, b_f, a_r, b_r, bsz)
            h = _lru_out(h_f, h_r, gu, lru_w_out[j].astype(BF16), h, _row(ln_mix_g[i]), _row(ln_mix_b[i]))
        else:
            hg = _matmul_glu(h, conf_w_pw1[j].astype(BF16), _row(conf_b_pw1[j]))
            hc = _dwconv_ln_silu(hg, conf_dw_w[j], _row(conf_dw_b[j]), _row(conf_ln_g[j]), _row(conf_ln_b[j]), bsz)
            h = _matmul_res_ln(hc, conf_w_pw2[j].astype(BF16), _row(conf_b_pw2[j]), h,
                               _row(ln_mix_g[i]), _row(ln_mix_b[i]))
        h = _peer_ffn(h, peer_w_query[i], peer_sub_keys[i], peer_u[i], peer_v[i],
                      _row(ln_ffn_g[i]), _row(ln_ffn_b[i]))
    out = h.reshape(s_tot, bsz, d)[N_META_TOKENS:]
    return jnp.transpose(out, (1, 0, 2))
```

```python
import functools

import jax
import jax.numpy as jnp
import numpy as np
from jax import lax
from jax.experimental import pallas as pl
from jax.experimental.pallas import tpu as pltpu

F32 = jnp.float32
BF16 = jnp.bfloat16
HIGHEST = lax.Precision.HIGHEST

N_META_TOKENS = 16
LRU_C = 8.0
LN_EPS = 1e-5
DEPTH = 2
DEEPNORM_ALPHA = (2.0 * DEPTH) ** 0.25

SUBLANES = 8
LANES = 128
VMEM_LIMIT_BYTES = 56 * 1024 * 1024

TOKEN_TILE = 512
SCAN_STEPS = 48
SCAN_CH = 256
PEER_TOKEN_TILE = 128
PEER_TOPK = 16


def _params(semantics):
    return pltpu.CompilerParams(dimension_semantics=semantics, vmem_limit_bytes=VMEM_LIMIT_BYTES)


def _mxu(a, b):
    return jnp.dot(a.astype(BF16), b.astype(BF16), preferred_element_type=F32)


def _layer_norm(y, g, b):
    mu = jnp.mean(y, axis=-1, keepdims=True)
    d = y - mu
    var = jnp.mean(d * d, axis=-1, keepdims=True)
    return d * lax.rsqrt(var + LN_EPS) * g + b


def _expm1(x):
    series = x * (1.0 + x * (1 / 2 + x * (1 / 6 + x * (1 / 24 + x * (1 / 120 + x * (1 / 720))))))
    return jnp.where(jnp.abs(x) < 0.1, series, jnp.exp(x) - 1.0)


def _row_spec(tm, n, col=0):
    return pl.BlockSpec((tm, n), lambda i: (i, col))


def _full_spec(shape):
    nd = len(shape)
    return pl.BlockSpec(shape, lambda i: (0,) * nd)


def _mm_kernel(x_ref, w_ref, o_ref):
    o_ref[...] = _mxu(x_ref[...], w_ref[...])


def _matmul(x, w):
    t, k = x.shape
    n = w.shape[1]
    return pl.pallas_call(
        _mm_kernel,
        grid=(t // TOKEN_TILE,),
        in_specs=[_row_spec(TOKEN_TILE, k), _full_spec((k, n))],
        out_specs=_row_spec(TOKEN_TILE, n),
        out_shape=jax.ShapeDtypeStruct((t, n), F32),
        compiler_params=_params(("parallel",)),
        name="matmul",
    )(x, w)


def _mm_glu_kernel(x_ref, w_ref, b_ref, o_ref):
    z = _mxu(x_ref[...], w_ref[...]) + b_ref[...]
    d = o_ref.shape[1]
    o_ref[...] = z[:, :d] * jax.nn.sigmoid(z[:, d:])


def _matmul_glu(x, w, b):
    t, k = x.shape
    n = w.shape[1]
    return pl.pallas_call(
        _mm_glu_kernel,
        grid=(t // TOKEN_TILE,),
        in_specs=[_row_spec(TOKEN_TILE, k), _full_spec((k, n)), _full_spec((1, n))],
        out_specs=_row_spec(TOKEN_TILE, n // 2),
        out_shape=jax.ShapeDtypeStruct((t, n // 2), F32),
        compiler_params=_params(("parallel",)),
        name="matmul_glu",
    )(x, w, b)


def _mm_res_ln_kernel(a_ref, w_ref, b_ref, res_ref, g_ref, beta_ref, o_ref):
    y = DEEPNORM_ALPHA * res_ref[...] + _mxu(a_ref[...], w_ref[...]) + b_ref[...]
    o_ref[...] = _layer_norm(y, g_ref[...], beta_ref[...])


def _matmul_res_ln(a, w, b, res, g, beta):
    t, k = a.shape
    n = w.shape[1]
    return pl.pallas_call(
        _mm_res_ln_kernel,
        grid=(t // TOKEN_TILE,),
        in_specs=[_row_spec(TOKEN_TILE, k), _full_spec((k, n)), _full_spec((1, n)),
                  _row_spec(TOKEN_TILE, n), _full_spec((1, n)), _full_spec((1, n))],
        out_specs=_row_spec(TOKEN_TILE, n),
        out_shape=jax.ShapeDtypeStruct((t, n), F32),
        compiler_params=_params(("parallel",)),
        name="matmul_res_ln",
    )(a, w, b, res, g, beta)


def _lru_coef_kernel(u_ref, prev_ref, next_ref, cw_ref, cb_ref, wa_ref, ba_ref, wx_ref, bx_ref,
                     lam_ref, af_ref, bf_ref, ar_ref, br_ref, *, batch):
    i = pl.program_id(0)
    last = pl.num_programs(0) - 1
    tm = u_ref.shape[0]
    prev = jnp.where(i > 0, prev_ref[...], 0.0)
    nxt = jnp.where(i < last, next_ref[...], 0.0)
    ext = jnp.concatenate([prev, u_ref[...], nxt], axis=0)
    uc = cb_ref[...] + sum(cw_ref[k:k + 1, :] * ext[k * batch:k * batch + tm, :] for k in range(4))
    n_blocks = wa_ref.shape[1]
    bw = wa_ref.shape[2]
    outs = ((af_ref, bf_ref), (ar_ref, br_ref))
    for n in range(n_blocks):
        sl = slice(n * bw, (n + 1) * bw)
        ub = uc[:, sl]
        ub16 = ub.astype(BF16)
        for d in range(2):
            r = jax.nn.sigmoid(jnp.dot(ub16, wa_ref[d, n], preferred_element_type=F32) + ba_ref[d:d + 1, sl])
            g = jax.nn.sigmoid(jnp.dot(ub16, wx_ref[d, n], preferred_element_type=F32) + bx_ref[d:d + 1, sl])
            log_a = (-LRU_C * r) * jax.nn.softplus(-lam_ref[d:d + 1, sl])
            a_ref, b_ref = outs[d]
            a_ref[:, sl] = jnp.exp(log_a)
            b_ref[:, sl] = jnp.sqrt(-_expm1(2.0 * log_a)) * (g * ub)


def _lru_coefs(gu, conv_w, conv_b, w_a, b_a, w_x, b_x, lam, batch):
    t = gu.shape[0]
    d = conv_w.shape[1]
    tm = TOKEN_TILE
    n_prev = tm // (2 * batch)
    n_next = tm // batch
    last_next = t // batch - 1
    out = jax.ShapeDtypeStruct((t, d), F32)
    return pl.pallas_call(
        functools.partial(_lru_coef_kernel, batch=batch),
        grid=(t // tm,),
        in_specs=[
            pl.BlockSpec((tm, d), lambda i: (i, 1)),
            pl.BlockSpec((2 * batch, d), lambda i: (jnp.maximum(i * n_prev - 1, 0), 1)),
            pl.BlockSpec((batch, d), lambda i: (jnp.minimum((i + 1) * n_next, last_next), 1)),
            _full_spec(conv_w.shape), _full_spec((1, d)),
            _full_spec(w_a.shape), _full_spec(b_a.shape),
            _full_spec(w_x.shape), _full_spec(b_x.shape), _full_spec(lam.shape),
        ],
        out_specs=[_row_spec(tm, d)] * 4,
        out_shape=[out] * 4,
        compiler_params=_params(("parallel",)),
        name="lru_coefs",
    )(gu, gu, gu, conv_w, conv_b, w_a, b_a, w_x, b_x, lam)


def _lru_scan_kernel(af_ref, bf_ref, ar_ref, br_ref, hf_ref, hr_ref, cf_ref, cr_ref, *, batch):
    j = pl.program_id(1)

    @pl.when(j == 0)
    def _():
        cf_ref[...] = jnp.zeros_like(cf_ref)
        cr_ref[...] = jnp.zeros_like(cr_ref)

    steps = af_ref.shape[0] // batch

    def body(s, carry):
        hf, hr = carry
        rf = pl.multiple_of(s * batch, batch)
        hf = af_ref[pl.ds(rf, batch), :] * hf + bf_ref[pl.ds(rf, batch), :]
        hf_ref[pl.ds(rf, batch), :] = hf
        rr = pl.multiple_of((steps - 1 - s) * batch, batch)
        hr = ar_ref[pl.ds(rr, batch), :] * hr + br_ref[pl.ds(rr, batch), :]
        hr_ref[pl.ds(rr, batch), :] = hr
        return hf, hr

    hf, hr = lax.fori_loop(0, steps, body, (cf_ref[...], cr_ref[...]), unroll=4)
    cf_ref[...] = hf
    cr_ref[...] = hr


def _lru_scan(a_f, b_f, a_r, b_r, batch):
    t, d = a_f.shape
    rows = SCAN_STEPS * batch
    n_chunks = t // rows
    fwd = pl.BlockSpec((rows, SCAN_CH), lambda c, j: (j, c))
    rev = pl.BlockSpec((rows, SCAN_CH), lambda c, j: (n_chunks - 1 - j, c))
    out = jax.ShapeDtypeStruct((t, d), F32)
    return pl.pallas_call(
        functools.partial(_lru_scan_kernel, batch=batch),
        grid=(d // SCAN_CH, n_chunks),
        in_specs=[fwd, fwd, rev, rev],
        out_specs=[fwd, rev],
        out_shape=[out, out],
        scratch_shapes=[pltpu.VMEM((batch, SCAN_CH), F32), pltpu.VMEM((batch, SCAN_CH), F32)],
        compiler_params=_params(("parallel", "arbitrary")),
        name="lru_scan",
    )(a_f, b_f, a_r, b_r)


def _lru_out_kernel(hf_ref, hr_ref, gate_ref, w_ref, res_ref, g_ref, beta_ref, o_ref):
    a = (hf_ref[...] + hr_ref[...]) * jax.nn.gelu(gate_ref[...])
    y = DEEPNORM_ALPHA * res_ref[...] + _mxu(a, w_ref[...])
    o_ref[...] = _layer_norm(y, g_ref[...], beta_ref[...])


def _lru_out(h_f, h_r, gu, w_out, res, g, beta):
    t, d = h_f.shape
    n = w_out.shape[1]
    tm = TOKEN_TILE
    return pl.pallas_call(
        _lru_out_kernel,
        grid=(t // tm,),
        in_specs=[_row_spec(tm, d), _row_spec(tm, d), _row_spec(tm, d, col=0), _full_spec((d, n)),
                  _row_spec(tm, n), _full_spec((1, n)), _full_spec((1, n))],
        out_specs=_row_spec(tm, n),
        out_shape=jax.ShapeDtypeStruct((t, n), F32),
        compiler_params=_params(("parallel",)),
        name="lru_out",
    )(h_f, h_r, gu, w_out, res, g, beta)


def _dwconv_kernel(cur_ref, prev_ref, next_ref, w_ref, b_ref, g_ref, beta_ref, o_ref, ext_ref, *, batch):
    i = pl.program_id(0)
    last = pl.num_programs(0) - 1
    tm = cur_ref.shape[0]
    taps = w_ref.shape[0]
    halo = (taps // 2) * batch
    ext_ref[0:halo, :] = jnp.where(i > 0, prev_ref[tm - halo:tm, :], 0.0)
    ext_ref[halo:halo + tm, :] = cur_ref[...]
    ext_ref[halo + tm:halo + tm + halo, :] = jnp.where(i < last, next_ref[0:halo, :], 0.0)

    def body(c, _):
        r0 = pl.multiple_of(c * batch, batch)
        acc = jnp.zeros((batch, cur_ref.shape[1]), F32) + b_ref[...]
        for k in range(taps):
            acc = acc + w_ref[k:k + 1, :] * ext_ref[pl.ds(r0 + k * batch, batch), :]
        y = _layer_norm(acc, g_ref[...], beta_ref[...])
        o_ref[pl.ds(r0, batch), :] = y * jax.nn.sigmoid(y)
        return 0

    lax.fori_loop(0, tm // batch, body, 0, unroll=2)


def _dwconv_ln_silu(x, w, b, g, beta, batch):
    t, d = x.shape
    tm = TOKEN_TILE
    n_tiles = t // tm
    halo = (w.shape[0] // 2) * batch
    assert halo <= tm
    return pl.pallas_call(
        functools.partial(_dwconv_kernel, batch=batch),
        grid=(n_tiles,),
        in_specs=[
            _row_spec(tm, d),
            pl.BlockSpec((tm, d), lambda i: (jnp.maximum(i - 1, 0), 0)),
            pl.BlockSpec((tm, d), lambda i: (jnp.minimum(i + 1, n_tiles - 1), 0)),
            _full_spec(w.shape), _full_spec((1, d)), _full_spec((1, d)), _full_spec((1, d)),
        ],
        out_specs=_row_spec(tm, d),
        out_shape=jax.ShapeDtypeStruct((t, d), F32),
        scratch_shapes=[pltpu.VMEM((tm + 2 * halo, d), F32)],
        compiler_params=_params(("parallel",)),
        name="dwconv_ln_silu",
    )(x, x, x, w, b, g, beta)


def _tree(tiles, op):
    tiles = list(tiles)
    while len(tiles) > 1:
        tiles = [op(tiles[i], tiles[i + 1]) for i in range(0, len(tiles) - 1, 2)] + tiles[len(tiles) & ~1:]
    return tiles[0]


def _all_rows(tiles, op):
    x = _tree(tiles, op)
    for shift in (4, 2, 1):
        x = op(x, pltpu.roll(x, shift, axis=0))
    return x


def _rows_of(tiles8, sub):
    out = tiles8[SUBLANES - 1]
    for i in range(SUBLANES - 2, -1, -1):
        out = jnp.where(sub == i, tiles8[i], out)
    return out


def _top16_rows(s, row_id):
    vals, ids = [], []
    not_found = float(len(s) * SUBLANES)
    for _ in range(PEER_TOPK):
        m = _all_rows(s, jnp.maximum)
        am = _all_rows([jnp.where(x == m, i, not_found) for x, i in zip(s, row_id)], jnp.minimum)
        s = [jnp.where(i == am, -jnp.inf, x) for x, i in zip(s, row_id)]
        vals.append(m)
        ids.append(am)
    return vals, ids


def _candidates(r1, r2, sub):
    lo4 = sub < 4
    r2_lo, r2_hi, r1_hi = _rows_of(r2[:8], sub), _rows_of(r2[8:], sub), _rows_of(r1[8:], sub)
    r2_dup4 = jnp.where(lo4, r2_lo, pltpu.roll(r2_lo, 4, axis=0))
    left = [r1[0], r1[0], r1[1], r1[2], r1[3], jnp.where(lo4, r1[4], r1[5]), jnp.where(lo4, r1[6], r1[7]), r1_hi]
    right = [r2_lo, r2_hi, r2_lo, r2_lo, r2_lo, r2_dup4, r2_dup4, r2[0]]
    return left, right


def _cand_flat(sub):
    subf = sub.astype(F32)
    lo4 = sub < 4
    k = float(PEER_TOPK)
    return [subf, 8.0 + subf, k + subf, 2 * k + subf, 3 * k + subf,
            jnp.where(lo4, 4 * k + subf, 5 * k + subf - 4.0), jnp.where(lo4, 6 * k + subf, 7 * k + subf - 4.0),
            (8.0 + subf) * k]


def _route_kernel(q_ref, keys_ref, off_ref, gate_ref, cnt_ref, *, n_keys):
    half = keys_ref.shape[2]
    keys_log2 = n_keys.bit_length() - 1
    n_heads = q_ref.shape[1] // (2 * half)
    n_tiles = LANES // SUBLANES
    sub = lax.broadcasted_iota(jnp.int32, (SUBLANES, LANES), 0)
    subf = sub.astype(F32)
    key_id = [subf + float(v * SUBLANES) for v in range(n_keys // SUBLANES)]
    flat = _cand_flat(sub)
    contract_last = (((1,), (1,)), ((), ()))
    n_lo = jnp.zeros((SUBLANES, LANES), F32)
    n_hi = jnp.zeros((SUBLANES, LANES), F32)
    off_tiles = [jnp.zeros((SUBLANES, LANES), jnp.int32) for _ in range(n_tiles)]
    gate_tiles = [jnp.zeros((SUBLANES, LANES), F32) for _ in range(n_tiles)]
    for h in range(n_heads):
        tops = []
        for p in range(2):
            qp = q_ref[:, (2 * h + p) * half:(2 * h + p + 1) * half].astype(BF16)
            st = lax.dot_general(keys_ref[p], qp, contract_last, preferred_element_type=F32)
            tops.append(_top16_rows([st[v * SUBLANES:(v + 1) * SUBLANES, :] for v in range(n_keys // SUBLANES)], key_id))
        (v1, i1), (v2, i2) = tops
        cand = [a + b for a, b in zip(*_candidates(v1, v2, sub))]
        expert = [a * float(n_keys) + b for a, b in zip(*_candidates(i1, i2, sub))]
        s_rank, e_rank = [], []
        for _ in range(PEER_TOPK):
            m = _all_rows(cand, jnp.maximum)
            am = _all_rows([jnp.where(c == m, f, 256.0) for c, f in zip(cand, flat)], jnp.minimum)
            sel = [f == am for f in flat]
            e_rank.append(_all_rows([jnp.where(s, e, -1.0) for s, e in zip(sel, expert)], jnp.maximum))
            cand = [jnp.where(s, -jnp.inf, c) for s, c in zip(sel, cand)]
            s_rank.append(m)
        p = [jnp.exp(s - s_rank[0]) for s in s_rank]
        den = _tree(p, jnp.add)
        for r in range(PEER_TOPK):
            e = e_rank[r].astype(jnp.int32)
            i1 = e >> keys_log2
            i2 = e & (n_keys - 1)
            second_half = ((i1 ^ i2) & 1) == 1
            pos = jnp.where(second_half, float(LANES - 1) - n_hi, n_lo)
            n_hi = n_hi + jnp.where(second_half, 1.0, 0.0)
            n_lo = n_lo + jnp.where(second_half, 0.0, 1.0)
            off = (i1 * (n_keys // 2) + (i2 >> 1)) * SUBLANES
            gate = p[r] / den
            gate = jnp.where(second_half, -gate, gate)
            for v in range(n_tiles):
                here = key_id[v] == pos
                off_tiles[v] = jnp.where(here, off, off_tiles[v])
                gate_tiles[v] = jnp.where(here, gate, gate_tiles[v])
    for v in range(n_tiles):
        rows = slice(v * SUBLANES, (v + 1) * SUBLANES)
        off_ref[rows, :] = off_tiles[v]
        gate_ref[rows, :] = gate_tiles[v]
    cnt_ref[...] = n_lo.astype(jnp.int32)


def _peer_route(q, keys, n_heads):
    t = q.shape[0]
    tb = LANES
    n_keys = keys.shape[1]
    assert n_heads * PEER_TOPK == LANES and n_keys == LANES and q.shape[1] == n_heads * 2 * keys.shape[2]
    out_spec = pl.BlockSpec((LANES, tb), lambda i: (0, i))
    return pl.pallas_call(
        functools.partial(_route_kernel, n_keys=n_keys),
        grid=(t // tb,),
        in_specs=[_row_spec(tb, q.shape[1]), _full_spec(keys.shape)],
        out_specs=[out_spec, out_spec, pl.BlockSpec((SUBLANES, tb), lambda i: (0, i))],
        out_shape=[jax.ShapeDtypeStruct((LANES, t), jnp.int32), jax.ShapeDtypeStruct((LANES, t), F32),
                   jax.ShapeDtypeStruct((SUBLANES, t), jnp.int32)],
        compiler_params=_params(("parallel",)),
        name="peer_route",
    )(q, keys)


_SLOT_OF_ROW = (0, 4, 2, 6, 1, 5, 3, 7)


def _sublane_sums(p, sub):
    lo4 = sub < 4
    c = []
    for a, b in ((p[0], p[1]), (p[2], p[3]), (p[4], p[5]), (p[6], p[7])):
        x = jnp.where(lo4, a, b)
        y = jnp.where(lo4, b, a)
        c.append(x + pltpu.roll(y, 4, axis=0))
    m2 = (sub & 2) == 0
    d = []
    for a, b in ((c[0], c[1]), (c[2], c[3])):
        d.append(jnp.where(m2, a + pltpu.roll(a, SUBLANES - 2, axis=0), b + pltpu.roll(b, 2, axis=0)))
    m1 = (sub & 1) == 0
    a, b = d
    return jnp.where(m1, a + pltpu.roll(a, SUBLANES - 1, axis=0), b + pltpu.roll(b, 1, axis=0))


def _lane_sums(r):
    hi = r.astype(BF16)
    lo = (r - hi.astype(F32)).astype(BF16)
    ones = jnp.ones((2 * LANES, LANES), BF16)
    return jnp.dot(jnp.concatenate([hi, lo], axis=1), ones, preferred_element_type=F32)


def _table_row(tbl_ref, off):
    return tbl_ref[pl.ds(pl.multiple_of(off, SUBLANES), SUBLANES), :]


N_POS_TILES = LANES // SUBLANES
WINDOW_TILES = 9
REST_GROUPS = ((9, 16),)
TOKEN_GROUP = 8
DOT_GROUP = 2


def _gather_specs(tb):
    offs = [pl.BlockSpec((SUBLANES, tb), lambda i, k=k: (k, i), memory_space=pltpu.SMEM) for k in range(N_POS_TILES)]
    return offs + [pl.BlockSpec((SUBLANES, tb), lambda i: (0, i), memory_space=pltpu.SMEM)]


def _tile_groups(second_half):
    order = list(range(N_POS_TILES))[::-1] if second_half else list(range(N_POS_TILES))
    return order[:WINDOW_TILES], [order[a:b] for a, b in REST_GROUPS]


def _group_needed(cnt_ref, t, second_half, group_index):
    n_first = cnt_ref[0, t]
    n_mine = LANES - n_first if second_half else n_first
    return n_mine > REST_GROUPS[group_index][0] * SUBLANES


def _peer_dot_kernel(*refs, second_half):
    off_ref = refs[:N_POS_TILES]
    cnt_ref, x_ref, tbl_ref, o_ref, ra_ref, rb_ref = refs[N_POS_TILES:]
    tb = x_ref.shape[0]
    sub = lax.broadcasted_iota(jnp.int32, (SUBLANES, LANES), 0)
    eye = (lax.broadcasted_iota(jnp.int32, (LANES, LANES), 0) == lax.broadcasted_iota(jnp.int32, (LANES, LANES), 1))
    window, groups = _tile_groups(second_half)

    def tile(t, x, k):
        prods = [_table_row(tbl_ref, off_ref[k][_SLOT_OF_ROW[j], t]) * x for j in range(SUBLANES)]
        return _sublane_sums(prods, sub)

    def rows(j, k):
        return slice(j * LANES + k * SUBLANES, j * LANES + (k + 1) * SUBLANES)

    def gather_windows(g0, r_ref):
        xs = [x_ref[g0 + j] for j in range(DOT_GROUP)]
        for k in window:
            for j, x in enumerate(xs):
                r_ref[rows(j, k), :] = tile(g0 + j, x, k)
        for j in range(DOT_GROUP):
            for tiles in groups:
                for k in tiles:
                    r_ref[rows(j, k), :] = jnp.zeros((SUBLANES, LANES), F32)

    def gather_rest(g0, r_ref):
        for j in range(DOT_GROUP):
            for gi, tiles in enumerate(groups):
                @pl.when(_group_needed(cnt_ref, g0 + j, second_half, gi))
                def _(j=j, tiles=tiles):
                    x = x_ref[g0 + j]
                    for k in tiles:
                        r_ref[rows(j, k), :] = tile(g0 + j, x, k)

    def finish(g0, r_ref):
        for j in range(DOT_GROUP):
            sums = jnp.sum(r_ref[j * LANES:(j + 1) * LANES, :], axis=1, keepdims=True)
            o_ref[pl.ds(g0 + j, 1), :] = jnp.sum(jnp.where(eye, sums, 0.0), axis=0, keepdims=True)

    rb_ref[...] = jnp.zeros_like(rb_ref)

    def group_pair(i, _):
        g0 = 2 * DOT_GROUP * i
        g1 = g0 + DOT_GROUP
        finish(jnp.maximum(g0 - DOT_GROUP, 0), rb_ref)
        gather_windows(g0, ra_ref)
        gather_rest(g0, ra_ref)
        finish(g0, ra_ref)
        gather_windows(g1, rb_ref)
        gather_rest(g1, rb_ref)
        return 0

    lax.fori_loop(0, tb // (2 * DOT_GROUP), group_pair, 0)
    finish(tb - DOT_GROUP, rb_ref)


def _peer_dots(offs_t, count, x3, tbl, second_half):
    tb = PEER_TOKEN_TILE
    t = x3.shape[0]
    return pl.pallas_call(
        functools.partial(_peer_dot_kernel, second_half=second_half),
        grid=(t // tb,),
        in_specs=_gather_specs(tb) + [pl.BlockSpec((tb, SUBLANES, LANES), lambda i: (i, 0, 0)),
                                      pl.BlockSpec(memory_space=pltpu.VMEM)],
        out_specs=pl.BlockSpec((tb, LANES), lambda i: (i, 0)),
        out_shape=jax.ShapeDtypeStruct((t, LANES), F32),
        scratch_shapes=[pltpu.VMEM((DOT_GROUP * LANES, LANES), F32)] * 2,
        compiler_params=_params(("parallel",)),
        name="peer_dots",
    )(*([offs_t] * N_POS_TILES), count, x3, tbl)


def _peer_weight_kernel(d0_ref, d1_ref, gate_ref, w0_ref, w1_ref):
    first = gate_ref[...] > 0.0
    w = jnp.abs(gate_ref[...]) * jax.nn.gelu(jnp.where(first, d0_ref[...], d1_ref[...]))
    w0_ref[...] = jnp.where(first, w, 0.0)
    w1_ref[...] = jnp.where(first, 0.0, w)


def _peer_weights(d0, d1, gates):
    t = d0.shape[0]
    tm = TOKEN_TILE
    out = jax.ShapeDtypeStruct((t, LANES), F32)
    return pl.pallas_call(
        _peer_weight_kernel,
        grid=(t // tm,),
        in_specs=[_row_spec(tm, LANES)] * 3,
        out_specs=[_row_spec(tm, LANES)] * 2,
        out_shape=[out, out],
        compiler_params=_params(("parallel",)),
        name="peer_weights",
    )(d0, d1, gates)


def _peer_mix_kernel(*refs, second_half):
    off_ref = refs[:N_POS_TILES]
    cnt_ref, w_ref, tbl_ref, o_ref, wa_ref, wb_ref = refs[N_POS_TILES:]
    tb = o_ref.shape[0]
    n_acc = 4
    eye = (lax.broadcasted_iota(jnp.int32, (LANES, LANES), 0) == lax.broadcasted_iota(jnp.int32, (LANES, LANES), 1))
    window, groups = _tile_groups(second_half)

    def spread(g0, dst_ref):
        w = w_ref[pl.ds(pl.multiple_of(g0, TOKEN_GROUP), TOKEN_GROUP), :]
        diag = [jnp.where(eye, jnp.broadcast_to(w[j:j + 1, :], (LANES, LANES)), 0.0) for j in range(TOKEN_GROUP)]
        dst_ref[...] = _lane_sums(jnp.concatenate(diag, axis=0))

    def weighted_rows(t, j, src_ref, tiles):
        acc = [jnp.zeros((SUBLANES, LANES), F32) for _ in range(n_acc)]
        for k in tiles:
            for s in range(SUBLANES):
                p = k * SUBLANES + s
                row = j * LANES + p
                wv = jnp.broadcast_to(src_ref[row:row + 1, :], (SUBLANES, LANES))
                acc[p % n_acc] = acc[p % n_acc] + wv * _table_row(tbl_ref, off_ref[k][s, t])
        return (acc[0] + acc[1]) + (acc[2] + acc[3])

    def gather(g0, src_ref, then=None):
        for j in range(TOKEN_GROUP):
            o_ref[g0 + j] = weighted_rows(g0 + j, j, src_ref, window)
        if then is not None:
            then()
        for j in range(TOKEN_GROUP):
            for gi, tiles in enumerate(groups):
                @pl.when(_group_needed(cnt_ref, g0 + j, second_half, gi))
                def _(j=j, tiles=tiles):
                    o_ref[g0 + j] = o_ref[g0 + j] + weighted_rows(g0 + j, j, src_ref, tiles)

    spread(0, wa_ref)

    def group_pair(i, _):
        g0 = 2 * TOKEN_GROUP * i
        g1 = g0 + TOKEN_GROUP
        spread(g1, wb_ref)
        gather(g0, wa_ref)
        gather(g1, wb_ref, lambda: spread(g1 + TOKEN_GROUP, wa_ref))
        return 0

    n_pairs = tb // (2 * TOKEN_GROUP)
    lax.fori_loop(0, n_pairs - 1, group_pair, 0)
    spread(tb - TOKEN_GROUP, wb_ref)
    gather(tb - 2 * TOKEN_GROUP, wa_ref)
    gather(tb - TOKEN_GROUP, wb_ref)


def _peer_mix(offs_t, count, w, tbl, second_half):
    tb = PEER_TOKEN_TILE
    t = w.shape[0]
    return pl.pallas_call(
        functools.partial(_peer_mix_kernel, second_half=second_half),
        grid=(t // tb,),
        in_specs=_gather_specs(tb) + [pl.BlockSpec((tb, LANES), lambda i: (i, 0)),
                                      pl.BlockSpec(memory_space=pltpu.VMEM)],
        out_specs=pl.BlockSpec((tb, SUBLANES, LANES), lambda i: (i, 0, 0)),
        out_shape=jax.ShapeDtypeStruct((t, SUBLANES, LANES), F32),
        scratch_shapes=[pltpu.VMEM((TOKEN_GROUP * LANES, LANES), F32)] * 2,
        compiler_params=_params(("parallel",)),
        name="peer_mix",
    )(*([offs_t] * N_POS_TILES), count, w, tbl)


def _add_ln_kernel(h_ref, f0_ref, f1_ref, g_ref, beta_ref, o_ref):
    y = DEEPNORM_ALPHA * h_ref[...] + (f0_ref[...] + f1_ref[...])
    o_ref[...] = _layer_norm(y, g_ref[...], beta_ref[...])


def _add_ln(h, f0, f1, g, beta):
    t, d = h.shape
    tm = TOKEN_TILE
    return pl.pallas_call(
        _add_ln_kernel,
        grid=(t // tm,),
        in_specs=[_row_spec(tm, d)] * 3 + [_full_spec((1, d))] * 2,
        out_specs=_row_spec(tm, d),
        out_shape=jax.ShapeDtypeStruct((t, d), F32),
        compiler_params=_params(("parallel",)),
        name="add_ln",
    )(h, f0, f1, g, beta)


def _split_by_parity(table, n_keys):
    d = table.shape[1]
    t5 = table.reshape(n_keys // 2, 2, n_keys // 2, 2, d)
    even = jnp.stack([t5[:, 0, :, 0], t5[:, 1, :, 1]], axis=1)
    odd = jnp.stack([t5[:, 0, :, 1], t5[:, 1, :, 0]], axis=1)
    rows = (n_keys * n_keys // 2) * (d // LANES)
    return even.reshape(rows, LANES), odd.reshape(rows, LANES)


def _peer_ffn(h, w_query, sub_keys, expert_u, expert_v, g, beta):
    t, d = h.shape
    n_experts = expert_u.shape[0]
    n_keys = sub_keys.shape[1]
    n_heads = w_query.shape[1] // (2 * sub_keys.shape[2])
    assert d == SUBLANES * LANES and n_keys * n_keys == n_experts and n_keys & (n_keys - 1) == 0
    q = _matmul(h, w_query.astype(BF16))
    offs_t, gates_t, count = _peer_route(q, sub_keys.astype(BF16), n_heads)
    gates = jnp.transpose(gates_t)
    x3 = h.reshape(t, SUBLANES, LANES)
    u_even, u_odd = _split_by_parity(expert_u, n_keys)
    v_even, v_odd = _split_by_parity(expert_v, n_keys)
    d0 = _peer_dots(offs_t, count, x3, u_even, False)
    d1 = _peer_dots(offs_t, count, x3, u_odd, True)
    w0, w1 = _peer_weights(d0, d1, gates)
    f0 = _peer_mix(offs_t, count, w0, v_even, False).reshape(t, d)
    f1 = _peer_mix(offs_t, count, w1, v_odd, True).reshape(t, d)
    return _add_ln(h, f0, f1, g, beta)


def _row(v):
    return v.reshape(1, -1)


def kernel(x, meta_tokens, lru_w_in, lru_conv_w, lru_conv_b, lru_w_a, lru_b_a, lru_w_x, lru_b_x, lru_lambda, lru_w_out, conf_w_pw1, conf_b_pw1, conf_dw_w, conf_dw_b, conf_ln_g, conf_ln_b, conf_w_pw2, conf_b_pw2, peer_w_query, peer_sub_keys, peer_u, peer_v, ln_mix_g, ln_mix_b, ln_ffn_g, ln_ffn_b):
    bsz, seq, d = x.shape
    s_tot = N_META_TOKENS + seq
    t = s_tot * bsz
    assert bsz % SUBLANES == 0 and t % TOKEN_TILE == 0 and s_tot % SCAN_STEPS == 0
    assert TOKEN_TILE % (2 * bsz) == 0 and t % PEER_TOKEN_TILE == 0
    meta = jnp.broadcast_to(meta_tokens.astype(x.dtype)[None], (bsz, N_META_TOKENS, d))
    h = jnp.concatenate([meta, x], axis=1)
    h = jnp.transpose(h, (1, 0, 2)).reshape(t, d)

    for i in range(DEPTH):
        j = i // 2
        if i % 2 == 0:
            gu = _matmul(h, lru_w_in[j].astype(BF16))
            a_f, b_f, a_r, b_r = _lru_coefs(
                gu, lru_conv_w[j], _row(lru_conv_b[j]), lru_w_a[j].astype(BF16), lru_b_a[j],
                lru_w_x[j].astype(BF16), lru_b_x[j], lru_lambda[j], bsz)
            h_f, h_r = _lru_scan(a_f, b_f, a_r, b_r, bsz)
            h = _lru_out(h_f, h_r, gu, lru_w_out[j].astype(BF16), h, _row(ln_mix_g[i]), _row(ln_mix_b[i]))
        else:
            hg = _matmul_glu(h, conf_w_pw1[j].astype(BF16), _row(conf_b_pw1[j]))
            hc = _dwconv_ln_silu(hg, conf_dw_w[j], _row(conf_dw_b[j]), _row(conf_ln_g[j]), _row(conf_ln_b[j]), bsz)
            h = _matmul_res_ln(hc, conf_w_pw2[j].astype(BF16), _row(conf_b_pw2[j]), h,
                               _row(ln_mix_g[i]), _row(ln_mix_b[i]))
        h = _peer_ffn(h, peer_w_query[i], peer_sub_keys[i], peer_u[i], peer_v[i],
                      _row(ln_ffn_g[i]), _row(ln_ffn_b[i]))
    out = h.reshape(s_tot, bsz, d)[N_META_TOKENS:]
    return jnp.transpose(out, (1, 0, 2))
```

```python
import functools

import jax
import jax.numpy as jnp
import numpy as np
from jax import lax
from jax.experimental import pallas as pl
from jax.experimental.pallas import tpu as pltpu

F32 = jnp.float32
BF16 = jnp.bfloat16
HIGHEST = lax.Precision.HIGHEST

N_META_TOKENS = 16
LRU_C = 8.0
LN_EPS = 1e-5
DEPTH = 2
DEEPNORM_ALPHA = (2.0 * DEPTH) ** 0.25

SUBLANES = 8
LANES = 128
VMEM_LIMIT_BYTES = 56 * 1024 * 1024

TOKEN_TILE = 512
SCAN_STEPS = 48
SCAN_CH = 256
PEER_TOKEN_TILE = 128
PEER_TOPK = 16


def _params(semantics):
    return pltpu.CompilerParams(dimension_semantics=semantics, vmem_limit_bytes=VMEM_LIMIT_BYTES)


def _mxu(a, b):
    return jnp.dot(a.astype(BF16), b.astype(BF16), preferred_element_type=F32)


def _layer_norm(y, g, b):
    mu = jnp.mean(y, axis=-1, keepdims=True)
    d = y - mu
    var = jnp.mean(d * d, axis=-1, keepdims=True)
    return d * lax.rsqrt(var + LN_EPS) * g + b


def _expm1(x):
    series = x * (1.0 + x * (1 / 2 + x * (1 / 6 + x * (1 / 24 + x * (1 / 120 + x * (1 / 720))))))
    return jnp.where(jnp.abs(x) < 0.1, series, jnp.exp(x) - 1.0)


def _row_spec(tm, n, col=0):
    return pl.BlockSpec((tm, n), lambda i: (i, col))


def _full_spec(shape):
    nd = len(shape)
    return pl.BlockSpec(shape, lambda i: (0,) * nd)


def _mm_kernel(x_ref, w_ref, o_ref):
    o_ref[...] = _mxu(x_ref[...], w_ref[...])


def _matmul(x, w):
    t, k = x.shape
    n = w.shape[1]
    return pl.pallas_call(
        _mm_kernel,
        grid=(t // TOKEN_TILE,),
        in_specs=[_row_spec(TOKEN_TILE, k), _full_spec((k, n))],
        out_specs=_row_spec(TOKEN_TILE, n),
        out_shape=jax.ShapeDtypeStruct((t, n), F32),
        compiler_params=_params(("parallel",)),
        name="matmul",
    )(x, w)


def _mm_glu_kernel(x_ref, w_ref, b_ref, o_ref):
    z = _mxu(x_ref[...], w_ref[...]) + b_ref[...]
    d = o_ref.shape[1]
    o_ref[...] = z[:, :d] * jax.nn.sigmoid(z[:, d:])


def _matmul_glu(x, w, b):
    t, k = x.shape
    n = w.shape[1]
    return pl.pallas_call(
        _mm_glu_kernel,
        grid=(t // TOKEN_TILE,),
        in_specs=[_row_spec(TOKEN_TILE, k), _full_spec((k, n)), _full_spec((1, n))],
        out_specs=_row_spec(TOKEN_TILE, n // 2),
        out_shape=jax.ShapeDtypeStruct((t, n // 2), F32),
        compiler_params=_params(("parallel",)),
        name="matmul_glu",
    )(x, w, b)


def _mm_res_ln_kernel(a_ref, w_ref, b_ref, res_ref, g_ref, beta_ref, o_ref):
    y = DEEPNORM_ALPHA * res_ref[...] + _mxu(a_ref[...], w_ref[...]) + b_ref[...]
    o_ref[...] = _layer_norm(y, g_ref[...], beta_ref[...])


def _matmul_res_ln(a, w, b, res, g, beta):
    t, k = a.shape
    n = w.shape[1]
    return pl.pallas_call(
        _mm_res_ln_kernel,
        grid=(t // TOKEN_TILE,),
        in_specs=[_row_spec(TOKEN_TILE, k), _full_spec((k, n)), _full_spec((1, n)),
                  _row_spec(TOKEN_TILE, n), _full_spec((1, n)), _full_spec((1, n))],
        out_specs=_row_spec(TOKEN_TILE, n),
        out_shape=jax.ShapeDtypeStruct((t, n), F32),
        compiler_params=_params(("parallel",)),
        name="matmul_res_ln",
    )(a, w, b, res, g, beta)


def _lru_coef_kernel(u_ref, prev_ref, next_ref, cw_ref, cb_ref, wa_ref, ba_ref, wx_ref, bx_ref,
                     lam_ref, af_ref, bf_ref, ar_ref, br_ref, *, batch):
    i = pl.program_id(0)
    last = pl.num_programs(0) - 1
    tm = u_ref.shape[0]
    prev = jnp.where(i > 0, prev_ref[...], 0.0)
    nxt = jnp.where(i < last, next_ref[...], 0.0)
    ext = jnp.concatenate([prev, u_ref[...], nxt], axis=0)
    uc = cb_ref[...] + sum(cw_ref[k:k + 1, :] * ext[k * batch:k * batch + tm, :] for k in range(4))
    n_blocks = wa_ref.shape[1]
    bw = wa_ref.shape[2]
    outs = ((af_ref, bf_ref), (ar_ref, br_ref))
    for n in range(n_blocks):
        sl = slice(n * bw, (n + 1) * bw)
        ub = uc[:, sl]
        ub16 = ub.astype(BF16)
        for d in range(2):
            r = jax.nn.sigmoid(jnp.dot(ub16, wa_ref[d, n], preferred_element_type=F32) + ba_ref[d:d + 1, sl])
            g = jax.nn.sigmoid(jnp.dot(ub16, wx_ref[d, n], preferred_element_type=F32) + bx_ref[d:d + 1, sl])
            log_a = (-LRU_C * r) * jax.nn.softplus(-lam_ref[d:d + 1, sl])
            a_ref, b_ref = outs[d]
            a_ref[:, sl] = jnp.exp(log_a)
            b_ref[:, sl] = jnp.sqrt(-_expm1(2.0 * log_a)) * (g * ub)


def _lru_coefs(gu, conv_w, conv_b, w_a, b_a, w_x, b_x, lam, batch):
    t = gu.shape[0]
    d = conv_w.shape[1]
    tm = TOKEN_TILE
    n_prev = tm // (2 * batch)
    n_next = tm // batch
    last_next = t // batch - 1
    out = jax.ShapeDtypeStruct((t, d), F32)
    return pl.pallas_call(
        functools.partial(_lru_coef_kernel, batch=batch),
        grid=(t // tm,),
        in_specs=[
            pl.BlockSpec((tm, d), lambda i: (i, 1)),
            pl.BlockSpec((2 * batch, d), lambda i: (jnp.maximum(i * n_prev - 1, 0), 1)),
            pl.BlockSpec((batch, d), lambda i: (jnp.minimum((i + 1) * n_next, last_next), 1)),
            _full_spec(conv_w.shape), _full_spec((1, d)),
            _full_spec(w_a.shape), _full_spec(b_a.shape),
            _full_spec(w_x.shape), _full_spec(b_x.shape), _full_spec(lam.shape),
        ],
        out_specs=[_row_spec(tm, d)] * 4,
        out_shape=[out] * 4,
        compiler_params=_params(("parallel",)),
        name="lru_coefs",
    )(gu, gu, gu, conv_w, conv_b, w_a, b_a, w_x, b_x, lam)


def _lru_scan_kernel(af_ref, bf_ref, ar_ref, br_ref, hf_ref, hr_ref, cf_ref, cr_ref, *, batch):
    j = pl.program_id(1)

    @pl.when(j == 0)
    def _():
        cf_ref[...] = jnp.zeros_like(cf_ref)
        cr_ref[...] = jnp.zeros_like(cr_ref)

    steps = af_ref.shape[0] // batch

    def body(s, carry):
        hf, hr = carry
        rf = pl.multiple_of(s * batch, batch)
        hf = af_ref[pl.ds(rf, batch), :] * hf + bf_ref[pl.ds(rf, batch), :]
        hf_ref[pl.ds(rf, batch), :] = hf
        rr = pl.multiple_of((steps - 1 - s) * batch, batch)
        hr = ar_ref[pl.ds(rr, batch), :] * hr + br_ref[pl.ds(rr, batch), :]
        hr_ref[pl.ds(rr, batch), :] = hr
        return hf, hr

    hf, hr = lax.fori_loop(0, steps, body, (cf_ref[...], cr_ref[...]), unroll=4)
    cf_ref[...] = hf
    cr_ref[...] = hr


def _lru_scan(a_f, b_f, a_r, b_r, batch):
    t, d = a_f.shape
    rows = SCAN_STEPS * batch
    n_chunks = t // rows
    fwd = pl.BlockSpec((rows, SCAN_CH), lambda c, j: (j, c))
    rev = pl.BlockSpec((rows, SCAN_CH), lambda c, j: (n_chunks - 1 - j, c))
    out = jax.ShapeDtypeStruct((t, d), F32)
    return pl.pallas_call(
        functools.partial(_lru_scan_kernel, batch=batch),
        grid=(d // SCAN_CH, n_chunks),
        in_specs=[fwd, fwd, rev, rev],
        out_specs=[fwd, rev],
        out_shape=[out, out],
        scratch_shapes=[pltpu.VMEM((batch, SCAN_CH), F32), pltpu.VMEM((batch, SCAN_CH), F32)],
        compiler_params=_params(("parallel", "arbitrary")),
        name="lru_scan",
    )(a_f, b_f, a_r, b_r)


def _lru_out_kernel(hf_ref, hr_ref, gate_ref, w_ref, res_ref, g_ref, beta_ref, o_ref):
    a = (hf_ref[...] + hr_ref[...]) * jax.nn.gelu(gate_ref[...])
    y = DEEPNORM_ALPHA * res_ref[...] + _mxu(a, w_ref[...])
    o_ref[...] = _layer_norm(y, g_ref[...], beta_ref[...])


def _lru_out(h_f, h_r, gu, w_out, res, g, beta):
    t, d = h_f.shape
    n = w_out.shape[1]
    tm = TOKEN_TILE
    return pl.pallas_call(
        _lru_out_kernel,
        grid=(t // tm,),
        in_specs=[_row_spec(tm, d), _row_spec(tm, d), _row_spec(tm, d, col=0), _full_spec((d, n)),
                  _row_spec(tm, n), _full_spec((1, n)), _full_spec((1, n))],
        out_specs=_row_spec(tm, n),
        out_shape=jax.ShapeDtypeStruct((t, n), F32),
        compiler_params=_params(("parallel",)),
        name="lru_out",
    )(h_f, h_r, gu, w_out, res, g, beta)


def _dwconv_kernel(cur_ref, prev_ref, next_ref, w_ref, b_ref, g_ref, beta_ref, o_ref, ext_ref, *, batch):
    i = pl.program_id(0)
    last = pl.num_programs(0) - 1
    tm = cur_ref.shape[0]
    taps = w_ref.shape[0]
    halo = (taps // 2) * batch
    ext_ref[0:halo, :] = jnp.where(i > 0, prev_ref[tm - halo:tm, :], 0.0)
    ext_ref[halo:halo + tm, :] = cur_ref[...]
    ext_ref[halo + tm:halo + tm + halo, :] = jnp.where(i < last, next_ref[0:halo, :], 0.0)

    def body(c, _):
        r0 = pl.multiple_of(c * batch, batch)
        acc = jnp.zeros((batch, cur_ref.shape[1]), F32) + b_ref[...]
        for k in range(taps):
            acc = acc + w_ref[k:k + 1, :] * ext_ref[pl.ds(r0 + k * batch, batch), :]
        y = _layer_norm(acc, g_ref[...], beta_ref[...])
        o_ref[pl.ds(r0, batch), :] = y * jax.nn.sigmoid(y)
        return 0

    lax.fori_loop(0, tm // batch, body, 0, unroll=2)


def _dwconv_ln_silu(x, w, b, g, beta, batch):
    t, d = x.shape
    tm = TOKEN_TILE
    n_tiles = t // tm
    halo = (w.shape[0] // 2) * batch
    assert halo <= tm
    return pl.pallas_call(
        functools.partial(_dwconv_kernel, batch=batch),
        grid=(n_tiles,),
        in_specs=[
            _row_spec(tm, d),
            pl.BlockSpec((tm, d), lambda i: (jnp.maximum(i - 1, 0), 0)),
            pl.BlockSpec((tm, d), lambda i: (jnp.minimum(i + 1, n_tiles - 1), 0)),
            _full_spec(w.shape), _full_spec((1, d)), _full_spec((1, d)), _full_spec((1, d)),
        ],
        out_specs=_row_spec(tm, d),
        out_shape=jax.ShapeDtypeStruct((t, d), F32),
        scratch_shapes=[pltpu.VMEM((tm + 2 * halo, d), F32)],
        compiler_params=_params(("parallel",)),
        name="dwconv_ln_silu",
    )(x, x, x, w, b, g, beta)


def _tree(tiles, op):
    tiles = list(tiles)
    while len(tiles) > 1:
        tiles = [op(tiles[i], tiles[i + 1]) for i in range(0, len(tiles) - 1, 2)] + tiles[len(tiles) & ~1:]
    return tiles[0]


def _all_rows(tiles, op):
    x = _tree(tiles, op)
    for shift in (4, 2, 1):
        x = op(x, pltpu.roll(x, shift, axis=0))
    return x


def _rows_of(tiles8, sub):
    out = tiles8[SUBLANES - 1]
    for i in range(SUBLANES - 2, -1, -1):
        out = jnp.where(sub == i, tiles8[i], out)
    return out


def _top16_rows(s, row_id):
    vals, ids = [], []
    not_found = float(len(s) * SUBLANES)
    for _ in range(PEER_TOPK):
        m = _all_rows(s, jnp.maximum)
        am = _all_rows([jnp.where(x == m, i, not_found) for x, i in zip(s, row_id)], jnp.minimum)
        s = [jnp.where(i == am, -jnp.inf, x) for x, i in zip(s, row_id)]
        vals.append(m)
        ids.append(am)
    return vals, ids


def _candidates(r1, r2, sub):
    lo4 = sub < 4
    r2_lo, r2_hi, r1_hi = _rows_of(r2[:8], sub), _rows_of(r2[8:], sub), _rows_of(r1[8:], sub)
    r2_dup4 = jnp.where(lo4, r2_lo, pltpu.roll(r2_lo, 4, axis=0))
    left = [r1[0], r1[0], r1[1], r1[2], r1[3], jnp.where(lo4, r1[4], r1[5]), jnp.where(lo4, r1[6], r1[7]), r1_hi]
    right = [r2_lo, r2_hi, r2_lo, r2_lo, r2_lo, r2_dup4, r2_dup4, r2[0]]
    return left, right


def _cand_flat(sub):
    subf = sub.astype(F32)
    lo4 = sub < 4
    k = float(PEER_TOPK)
    return [subf, 8.0 + subf, k + subf, 2 * k + subf, 3 * k + subf,
            jnp.where(lo4, 4 * k + subf, 5 * k + subf - 4.0), jnp.where(lo4, 6 * k + subf, 7 * k + subf - 4.0),
            (8.0 + subf) * k]


def _route_kernel(q_ref, keys_ref, off_ref, gate_ref, cnt_ref, *, n_keys):
    half = keys_ref.shape[2]
    keys_log2 = n_keys.bit_length() - 1
    n_heads = q_ref.shape[1] // (2 * half)
    n_tiles = LANES // SUBLANES
    sub = lax.broadcasted_iota(jnp.int32, (SUBLANES, LANES), 0)
    subf = sub.astype(F32)
    key_id = [subf + float(v * SUBLANES) for v in range(n_keys // SUBLANES)]
    flat = _cand_flat(sub)
    contract_last = (((1,), (1,)), ((), ()))
    n_lo = jnp.zeros((SUBLANES, LANES), F32)
    n_hi = jnp.zeros((SUBLANES, LANES), F32)
    off_tiles = [jnp.zeros((SUBLANES, LANES), jnp.int32) for _ in range(n_tiles)]
    gate_tiles = [jnp.zeros((SUBLANES, LANES), F32) for _ in range(n_tiles)]
    for h in range(n_heads):
        tops = []
        for p in range(2):
            qp = q_ref[:, (2 * h + p) * half:(2 * h + p + 1) * half].astype(BF16)
            st = lax.dot_general(keys_ref[p], qp, contract_last, preferred_element_type=F32)
            tops.append(_top16_rows([st[v * SUBLANES:(v + 1) * SUBLANES, :] for v in range(n_keys // SUBLANES)], key_id))
        (v1, i1), (v2, i2) = tops
        cand = [a + b for a, b in zip(*_candidates(v1, v2, sub))]
        expert = [a * float(n_keys) + b for a, b in zip(*_candidates(i1, i2, sub))]
        s_rank, e_rank = [], []
        for _ in range(PEER_TOPK):
            m = _all_rows(cand, jnp.maximum)
            am = _all_rows([jnp.where(c == m, f, 256.0) for c, f in zip(cand, flat)], jnp.minimum)
            sel = [f == am for f in flat]
            e_rank.append(_all_rows([jnp.where(s, e, -1.0) for s, e in zip(sel, expert)], jnp.maximum))
            cand = [jnp.where(s, -jnp.inf, c) for s, c in zip(sel, cand)]
            s_rank.append(m)
        p = [jnp.exp(s - s_rank[0]) for s in s_rank]
        den = _tree(p, jnp.add)
        for r in range(PEER_TOPK):
            e = e_rank[r].astype(jnp.int32)
            i1 = e >> keys_log2
            i2 = e & (n_keys - 1)
            second_half = ((i1 ^ i2) & 1) == 1
            pos = jnp.where(second_half, float(LANES - 1) - n_hi, n_lo)
            n_hi = n_hi + jnp.where(second_half, 1.0, 0.0)
            n_lo = n_lo + jnp.where(second_half, 0.0, 1.0)
            off = (i1 * (n_keys // 2) + (i2 >> 1)) * SUBLANES
            gate = p[r] / den
            gate = jnp.where(second_half, -gate, gate)
            for v in range(n_tiles):
                here = key_id[v] == pos
                off_tiles[v] = jnp.where(here, off, off_tiles[v])
                gate_tiles[v] = jnp.where(here, gate, gate_tiles[v])
    for v in range(n_tiles):
        rows = slice(v * SUBLANES, (v + 1) * SUBLANES)
        off_ref[rows, :] = off_tiles[v]
        gate_ref[rows, :] = gate_tiles[v]
    cnt_ref[...] = n_lo.astype(jnp.int32)


def _peer_route(q, keys, n_heads):
    t = q.shape[0]
    tb = LANES
    n_keys = keys.shape[1]
    assert n_heads * PEER_TOPK == LANES and n_keys == LANES and q.shape[1] == n_heads * 2 * keys.shape[2]
    out_spec = pl.BlockSpec((LANES, tb), lambda i: (0, i))
    return pl.pallas_call(
        functools.partial(_route_kernel, n_keys=n_keys),
        grid=(t // tb,),
        in_specs=[_row_spec(tb, q.shape[1]), _full_spec(keys.shape)],
        out_specs=[out_spec, out_spec, pl.BlockSpec((SUBLANES, tb), lambda i: (0, i))],
        out_shape=[jax.ShapeDtypeStruct((LANES, t), jnp.int32), jax.ShapeDtypeStruct((LANES, t), F32),
                   jax.ShapeDtypeStruct((SUBLANES, t), jnp.int32)],
        compiler_params=_params(("parallel",)),
        name="peer_route",
    )(q, keys)


_SLOT_OF_ROW = (0, 4, 2, 6, 1, 5, 3, 7)


def _sublane_sums(p, sub):
    lo4 = sub < 4
    c = []
    for a, b in ((p[0], p[1]), (p[2], p[3]), (p[4], p[5]), (p[6], p[7])):
        x = jnp.where(lo4, a, b)
        y = jnp.where(lo4, b, a)
        c.append(x + pltpu.roll(y, 4, axis=0))
    m2 = (sub & 2) == 0
    d = []
    for a, b in ((c[0], c[1]), (c[2], c[3])):
        d.append(jnp.where(m2, a + pltpu.roll(a, SUBLANES - 2, axis=0), b + pltpu.roll(b, 2, axis=0)))
    m1 = (sub & 1) == 0
    a, b = d
    return jnp.where(m1, a + pltpu.roll(a, SUBLANES - 1, axis=0), b + pltpu.roll(b, 1, axis=0))


def _lane_sums(r):
    hi = r.astype(BF16)
    lo = (r - hi.astype(F32)).astype(BF16)
    ones = jnp.ones((2 * LANES, LANES), BF16)
    return jnp.dot(jnp.concatenate([hi, lo], axis=1), ones, preferred_element_type=F32)


def _table_row(tbl_ref, off):
    return tbl_ref[pl.ds(pl.multiple_of(off, SUBLANES), SUBLANES), :]


N_POS_TILES = LANES // SUBLANES
WINDOW_TILES = 9
REST_GROUPS = ((9, 16),)
TOKEN_GROUP = 8
DOT_GROUP = 2


def _gather_specs(tb):
    offs = [pl.BlockSpec((SUBLANES, tb), lambda i, k=k: (k, i), memory_space=pltpu.SMEM) for k in range(N_POS_TILES)]
    return offs + [pl.BlockSpec((SUBLANES, tb), lambda i: (0, i), memory_space=pltpu.SMEM)]


def _tile_groups(second_half):
    order = list(range(N_POS_TILES))[::-1] if second_half else list(range(N_POS_TILES))
    return order[:WINDOW_TILES], [order[a:b] for a, b in REST_GROUPS]


def _group_needed(cnt_ref, t, second_half, group_index):
    n_first = cnt_ref[0, t]
    n_mine = LANES - n_first if second_half else n_first
    return n_mine > REST_GROUPS[group_index][0] * SUBLANES


def _peer_dot_kernel(*refs, second_half):
    off_ref = refs[:N_POS_TILES]
    cnt_ref, x_ref, tbl_ref, o_ref, ra_ref, rb_ref = refs[N_POS_TILES:]
    tb = x_ref.shape[0]
    sub = lax.broadcasted_iota(jnp.int32, (SUBLANES, LANES), 0)
    eye = (lax.broadcasted_iota(jnp.int32, (LANES, LANES), 0) == lax.broadcasted_iota(jnp.int32, (LANES, LANES), 1))
    window, groups = _tile_groups(second_half)

    def tile(t, x, k):
        prods = [_table_row(tbl_ref, off_ref[k][_SLOT_OF_ROW[j], t]) * x for j in range(SUBLANES)]
        return _sublane_sums(prods, sub)

    def rows(j, k):
        return slice(j * LANES + k * SUBLANES, j * LANES + (k + 1) * SUBLANES)

    def gather(g0, r_ref):
        needed = [[_group_needed(cnt_ref, g0 + j, second_half, gi) for gi in range(len(groups))]
                  for j in range(DOT_GROUP)]
        for j in range(DOT_GROUP):
            x = x_ref[g0 + j]
            for k in window:
                r_ref[rows(j, k), :] = tile(g0 + j, x, k)
            for tiles in groups:
                for k in tiles:
                    r_ref[rows(j, k), :] = jnp.zeros((SUBLANES, LANES), F32)
        for j in range(DOT_GROUP):
            for gi, tiles in enumerate(groups):
                @pl.when(needed[j][gi])
                def _(j=j, tiles=tiles):
                    x = x_ref[g0 + j]
                    for k in tiles:
                        r_ref[rows(j, k), :] = tile(g0 + j, x, k)

    def finish(g0, r_ref):
        for j in range(DOT_GROUP):
            sums = jnp.sum(r_ref[j * LANES:(j + 1) * LANES, :], axis=1, keepdims=True)
            o_ref[pl.ds(g0 + j, 1), :] = jnp.sum(jnp.where(eye, sums, 0.0), axis=0, keepdims=True)

    rb_ref[...] = jnp.zeros_like(rb_ref)

    def group_pair(i, _):
        g0 = 2 * DOT_GROUP * i
        g1 = g0 + DOT_GROUP
        finish(jnp.maximum(g0 - DOT_GROUP, 0), rb_ref)
        gather(g0, ra_ref)
        finish(g0, ra_ref)
        gather(g1, rb_ref)
        return 0

    lax.fori_loop(0, tb // (2 * DOT_GROUP), group_pair, 0)
    finish(tb - DOT_GROUP, rb_ref)


def _peer_dots(offs_t, count, x3, tbl, second_half):
    tb = PEER_TOKEN_TILE
    t = x3.shape[0]
    return pl.pallas_call(
        functools.partial(_peer_dot_kernel, second_half=second_half),
        grid=(t // tb,),
        in_specs=_gather_specs(tb) + [pl.BlockSpec((tb, SUBLANES, LANES), lambda i: (i, 0, 0)),
                                      pl.BlockSpec(memory_space=pltpu.VMEM)],
        out_specs=pl.BlockSpec((tb, LANES), lambda i: (i, 0)),
        out_shape=jax.ShapeDtypeStruct((t, LANES), F32),
        scratch_shapes=[pltpu.VMEM((DOT_GROUP * LANES, LANES), F32)] * 2,
        compiler_params=_params(("parallel",)),
        name="peer_dots",
    )(*([offs_t] * N_POS_TILES), count, x3, tbl)


def _peer_weight_kernel(d0_ref, d1_ref, gate_ref, w0_ref, w1_ref):
    first = gate_ref[...] > 0.0
    w = jnp.abs(gate_ref[...]) * jax.nn.gelu(jnp.where(first, d0_ref[...], d1_ref[...]))
    w0_ref[...] = jnp.where(first, w, 0.0)
    w1_ref[...] = jnp.where(first, 0.0, w)


def _peer_weights(d0, d1, gates):
    t = d0.shape[0]
    tm = TOKEN_TILE
    out = jax.ShapeDtypeStruct((t, LANES), F32)
    return pl.pallas_call(
        _peer_weight_kernel,
        grid=(t // tm,),
        in_specs=[_row_spec(tm, LANES)] * 3,
        out_specs=[_row_spec(tm, LANES)] * 2,
        out_shape=[out, out],
        compiler_params=_params(("parallel",)),
        name="peer_weights",
    )(d0, d1, gates)


def _peer_mix_kernel(*refs, second_half):
    off_ref = refs[:N_POS_TILES]
    cnt_ref, w_ref, tbl_ref, o_ref, wa_ref, wb_ref = refs[N_POS_TILES:]
    tb = o_ref.shape[0]
    n_acc = 4
    eye = (lax.broadcasted_iota(jnp.int32, (LANES, LANES), 0) == lax.broadcasted_iota(jnp.int32, (LANES, LANES), 1))
    window, groups = _tile_groups(second_half)

    def spread(g0, dst_ref):
        w = w_ref[pl.ds(pl.multiple_of(g0, TOKEN_GROUP), TOKEN_GROUP), :]
        diag = [jnp.where(eye, jnp.broadcast_to(w[j:j + 1, :], (LANES, LANES)), 0.0) for j in range(TOKEN_GROUP)]
        dst_ref[...] = _lane_sums(jnp.concatenate(diag, axis=0))

    def weighted_rows(t, j, src_ref, tiles):
        acc = [jnp.zeros((SUBLANES, LANES), F32) for _ in range(n_acc)]
        for k in tiles:
            for s in range(SUBLANES):
                p = k * SUBLANES + s
                row = j * LANES + p
                wv = jnp.broadcast_to(src_ref[row:row + 1, :], (SUBLANES, LANES))
                acc[p % n_acc] = acc[p % n_acc] + wv * _table_row(tbl_ref, off_ref[k][s, t])
        return (acc[0] + acc[1]) + (acc[2] + acc[3])

    def gather(g0, src_ref, then=None):
        needed = [[_group_needed(cnt_ref, g0 + j, second_half, gi) for gi in range(len(groups))]
                  for j in range(TOKEN_GROUP)]
        for j in range(TOKEN_GROUP):
            o_ref[g0 + j] = weighted_rows(g0 + j, j, src_ref, window)
        if then is not None:
            then()
        for j in range(TOKEN_GROUP):
            for gi, tiles in enumerate(groups):
                @pl.when(needed[j][gi])
                def _(j=j, tiles=tiles):
                    o_ref[g0 + j] = o_ref[g0 + j] + weighted_rows(g0 + j, j, src_ref, tiles)

    spread(0, wa_ref)

    def group_pair(i, _):
        g0 = 2 * TOKEN_GROUP * i
        g1 = g0 + TOKEN_GROUP
        spread(g1, wb_ref)
        gather(g0, wa_ref)
        gather(g1, wb_ref, lambda: spread(g1 + TOKEN_GROUP, wa_ref))
        return 0

    n_pairs = tb // (2 * TOKEN_GROUP)
    lax.fori_loop(0, n_pairs - 1, group_pair, 0)
    spread(tb - TOKEN_GROUP, wb_ref)
    gather(tb - 2 * TOKEN_GROUP, wa_ref)
    gather(tb - TOKEN_GROUP, wb_ref)


def _peer_mix(offs_t, count, w, tbl, second_half):
    tb = PEER_TOKEN_TILE
    t = w.shape[0]
    return pl.pallas_call(
        functools.partial(_peer_mix_kernel, second_half=second_half),
        grid=(t // tb,),
        in_specs=_gather_specs(tb) + [pl.BlockSpec((tb, LANES), lambda i: (i, 0)),
                                      pl.BlockSpec(memory_space=pltpu.VMEM)],
        out_specs=pl.BlockSpec((tb, SUBLANES, LANES), lambda i: (i, 0, 0)),
        out_shape=jax.ShapeDtypeStruct((t, SUBLANES, LANES), F32),
        scratch_shapes=[pltpu.VMEM((TOKEN_GROUP * LANES, LANES), F32)] * 2,
        compiler_params=_params(("parallel",)),
        name="peer_mix",
    )(*([offs_t] * N_POS_TILES), count, w, tbl)


def _add_ln_kernel(h_ref, f0_ref, f1_ref, g_ref, beta_ref, o_ref):
    y = DEEPNORM_ALPHA * h_ref[...] + (f0_ref[...] + f1_ref[...])
    o_ref[...] = _layer_norm(y, g_ref[...], beta_ref[...])


def _add_ln(h, f0, f1, g, beta):
    t, d = h.shape
    tm = TOKEN_TILE
    return pl.pallas_call(
        _add_ln_kernel,
        grid=(t // tm,),
        in_specs=[_row_spec(tm, d)] * 3 + [_full_spec((1, d))] * 2,
        out_specs=_row_spec(tm, d),
        out_shape=jax.ShapeDtypeStruct((t, d), F32),
        compiler_params=_params(("parallel",)),
        name="add_ln",
    )(h, f0, f1, g, beta)


def _split_by_parity(table, n_keys):
    d = table.shape[1]
    t5 = table.reshape(n_keys // 2, 2, n_keys // 2, 2, d)
    even = jnp.stack([t5[:, 0, :, 0], t5[:, 1, :, 1]], axis=1)
    odd = jnp.stack([t5[:, 0, :, 1], t5[:, 1, :, 0]], axis=1)
    rows = (n_keys * n_keys // 2) * (d // LANES)
    return even.reshape(rows, LANES), odd.reshape(rows, LANES)


def _peer_ffn(h, w_query, sub_keys, expert_u, expert_v, g, beta):
    t, d = h.shape
    n_experts = expert_u.shape[0]
    n_keys = sub_keys.shape[1]
    n_heads = w_query.shape[1] // (2 * sub_keys.shape[2])
    assert d == SUBLANES * LANES and n_keys * n_keys == n_experts and n_keys & (n_keys - 1) == 0
    q = _matmul(h, w_query.astype(BF16))
    offs_t, gates_t, count = _peer_route(q, sub_keys.astype(BF16), n_heads)
    gates = jnp.transpose(gates_t)
    x3 = h.reshape(t, SUBLANES, LANES)
    u_even, u_odd = _split_by_parity(expert_u, n_keys)
    v_even, v_odd = _split_by_parity(expert_v, n_keys)
    d0 = _peer_dots(offs_t, count, x3, u_even, False)
    d1 = _peer_dots(offs_t, count, x3, u_odd, True)
    w0, w1 = _peer_weights(d0, d1, gates)
    f0 = _peer_mix(offs_t, count, w0, v_even, False).reshape(t, d)
    f1 = _peer_mix(offs_t, count, w1, v_odd, True).reshape(t, d)
    return _add_ln(h, f0, f1, g, beta)


def _row(v):
    return v.reshape(1, -1)


def kernel(x, meta_tokens, lru_w_in, lru_conv_w, lru_conv_b, lru_w_a, lru_b_a, lru_w_x, lru_b_x, lru_lambda, lru_w_out, conf_w_pw1, conf_b_pw1, conf_dw_w, conf_dw_b, conf_ln_g, conf_ln_b, conf_w_pw2, conf_b_pw2, peer_w_query, peer_sub_keys, peer_u, peer_v, ln_mix_g, ln_mix_b, ln_ffn_g, ln_ffn_b):
    bsz, seq, d = x.shape
    s_tot = N_META_TOKENS + seq
    t = s_tot * bsz
    assert bsz % SUBLANES == 0 and t % TOKEN_TILE == 0 and s_tot % SCAN_STEPS == 0
    assert TOKEN_TILE % (2 * bsz) == 0 and t % PEER_TOKEN_TILE == 0
    meta = jnp.broadcast_to(meta_tokens.astype(x.dtype)[None], (bsz, N_META_TOKENS, d))
    h = jnp.concatenate([meta, x], axis=1)
    h = jnp.transpose(h, (1, 0, 2)).reshape(t, d)

    for i in range(DEPTH):
        j = i // 2
        if i % 2 == 0:
            gu = _matmul(h, lru_w_in[j].astype(BF16))
            a_f, b_f, a_r, b_r = _lru_coefs(
                gu, lru_conv_w[j], _row(lru_conv_b[j]), lru_w_a[j].astype(BF16), lru_b_a[j],
                lru_w_x[j].astype(BF16), lru_b_x[j], lru_lambda[j], bsz)
            h_f, h_r = _lru_scan(a_f, b_f, a_r, b_r, bsz)
            h = _lru_out(h_f, h_r, gu, lru_w_out[j].astype(BF16), h, _row(ln_mix_g[i]), _row(ln_mix_b[i]))
        else:
            hg = _matmul_glu(h, conf_w_pw1[j].astype(BF16), _row(conf_b_pw1[j]))
            hc = _dwconv_ln_silu(hg, conf_dw_w[j], _row(conf_dw_b[j]), _row(conf_ln_g[j]), _row(conf_ln_b[j]), bsz)
            h = _matmul_res_ln(hc, conf_w_pw2[j].astype(BF16), _row(conf_b_pw2[j]), h,
                               _row(ln_mix_g[i]), _row(ln_mix_b[i]))
        h = _peer_ffn(h, peer_w_query[i], peer_sub_keys[i], peer_u[i], peer_v[i],
                      _row(ln_ffn_g[i]), _row(ln_ffn_b[i]))
    out = h.reshape(s_tot, bsz, d)[N_META_TOKENS:]
    return jnp.transpose(out, (1, 0, 2))
```
